```python
import math
import jax, jax.numpy as jnp
from jax import lax
import numpy as np

D_MODEL = 2048
BATCH = 4
SEQ = 4096
DEPTH = 4
DEC_BATCH = 16
DEC_SEQ = 64
PAST_LEN = 4096

CHUNK = 64
Q_BLOCK = 128
N_DIFF_HEADS = 4
DIFF_HEAD_DIM = 64
DIFF_V_DIM = 2 * DIFF_HEAD_DIM
DIFF_QK = N_DIFF_HEADS * 2 * DIFF_HEAD_DIM
DIFF_V = N_DIFF_HEADS * DIFF_V_DIM
N_FOX_HEADS = 4
FOX_HEAD_DIM = 128
FOX_W = N_FOX_HEADS * FOX_HEAD_DIM
FORGET_BIAS_INIT = 3.0
D_RNN = 1024
N_RNN_BLOCKS = 16
RNN_BLOCK = D_RNN // N_RNN_BLOCKS
CONV_WIDTH = 4
LRU_C = 8.0
N_BRANCHES = 3
D_FF = 5632
N_EXPERTS = 8
TOP_K = 2
D_FF_EXPERT = 2816
N_DENSE = (DEPTH + 1) // 2
N_MOE = DEPTH // 2
D_PLE = 256
EPS = 1e-6

PROJ_SIZES = (DIFF_QK, DIFF_QK, DIFF_V, FOX_W, FOX_W, FOX_W, N_FOX_HEADS, D_RNN, D_RNN, N_BRANCHES * D_MODEL)
N_IN = 2 * DIFF_QK + DIFF_V + 3 * FOX_W + N_FOX_HEADS + 2 * D_RNN + N_BRANCHES * D_MODEL
FORGET_OFF = 2 * DIFF_QK + DIFF_V + 3 * FOX_W

kernel_name = "hybrid_streaming_diff_fox_rglru_step"


def _rmsnorm(x, g):
    xf = x.astype(jnp.float32)
    y = xf * lax.rsqrt(jnp.mean(xf * xf, axis=-1, keepdims=True) + EPS)
    return (y * g.astype(jnp.float32)).astype(x.dtype)


def _split_proj(proj):
    idx = [int(v) for v in np.cumsum(PROJ_SIZES)[:-1]]
    return jnp.split(proj, idx, axis=-1)


def _query_blocks(fn, q_side):
    tq = q_side[0].shape[1]
    if tq <= Q_BLOCK or tq % Q_BLOCK != 0:
        return fn(*q_side)
    nb = tq // Q_BLOCK
    blocked = tuple(jnp.moveaxis(a.reshape(a.shape[0], nb, Q_BLOCK, *a.shape[2:]), 1, 0) for a in q_side)
    out = lax.map(lambda args: fn(*args), blocked)
    out = jnp.moveaxis(out, 0, 1)
    return out.reshape(out.shape[0], tq, *out.shape[3:])


def _diff_attention(q, k, v, q_pos, k_pos, lam, lam_init, gain):
    B, Tq = q.shape[0], q.shape[1]
    slopes = 2.0 ** (-8.0 * jnp.arange(1, N_DIFF_HEADS + 1, dtype=jnp.float32) / N_DIFF_HEADS)
    k1, k2 = k[..., :DIFF_HEAD_DIM], k[..., DIFF_HEAD_DIM:]
    scale = DIFF_HEAD_DIM ** -0.5

    def block(qb, qpb):
        dist = jnp.abs(qpb[:, :, None] - k_pos[None, None, :]).astype(jnp.float32)
        allowed = (k_pos // CHUNK)[None, None, :] <= (qpb // CHUNK)[:, :, None]
        bias = jnp.where(allowed[:, None], -slopes[None, :, None, None] * dist[:, None], -jnp.inf)
        s1 = jnp.einsum('bqhd,bkhd->bhqk', qb[..., :DIFF_HEAD_DIM], k1).astype(jnp.float32) * scale + bias
        s2 = jnp.einsum('bqhd,bkhd->bhqk', qb[..., DIFF_HEAD_DIM:], k2).astype(jnp.float32) * scale + bias
        attn = jax.nn.softmax(s1, axis=-1) - lam * jax.nn.softmax(s2, axis=-1)
        return jnp.einsum('bhqk,bkhd->bqhd', attn.astype(v.dtype), v)

    o = _query_blocks(block, (q, q_pos[None]))
    o = _rmsnorm(o, gain) * (1.0 - lam_init)
    return o.reshape(B, Tq, DIFF_V)


def _fox_attention(q, k, v, cum_q, cum_k, q_pos, k_pos):
    B, Tq = q.shape[0], q.shape[1]
    scale = FOX_HEAD_DIM ** -0.5
    cum_kh = jnp.moveaxis(cum_k, 2, 1)

    def block(qb, cqb, qpb):
        decay = jnp.moveaxis(cqb, 2, 1)[..., None] - cum_kh[:, :, None, :]
        s = jnp.einsum('bqhd,bkhd->bhqk', qb, k).astype(jnp.float32) * scale + decay
        allowed = k_pos[None, None, :] <= qpb[:, :, None]
        s = jnp.where(allowed[:, None], s, -jnp.inf)
        return jnp.einsum('bhqk,bkhd->bqhd', jax.nn.softmax(s, axis=-1).astype(v.dtype), v)

    o = _query_blocks(block, (q, cum_q, q_pos[None]))
    return o.reshape(B, Tq, FOX_W)


def _rglru(xr, conv_hist, h0, conv_w, conv_b, w_a, b_a, w_i, b_i, lam):
    B, T, _ = xr.shape
    xp = jnp.concatenate([conv_hist.astype(xr.dtype), xr], axis=1)
    xc = conv_b
    for j in range(CONV_WIDTH):
        xc = xc + xp[:, j:j + T] * conv_w[j]
    xb = xc.reshape(B, T, N_RNN_BLOCKS, RNN_BLOCK)
    r = jax.nn.sigmoid(jnp.einsum('btni,nij->btnj', xb, w_a).reshape(B, T, D_RNN) + b_a)
    ig = jax.nn.sigmoid(jnp.einsum('btni,nij->btnj', xb, w_i).reshape(B, T, D_RNN) + b_i)
    log_a = -LRU_C * r.astype(jnp.float32) * jax.nn.softplus(-lam.astype(jnp.float32))
    a = jnp.exp(log_a)
    b = jnp.sqrt(-jnp.expm1(2.0 * log_a)) * (ig * xc).astype(jnp.float32)
    b = b.at[:, 0].add(a[:, 0] * h0.astype(jnp.float32))

    def comb(l, rr):
        return (l[0] * rr[0], rr[0] * l[1] + rr[1])

    _, h = lax.associative_scan(comb, (a, b), axis=1)
    return h.astype(xr.dtype), h[:, -1], xp[:, -(CONV_WIDTH - 1):]


def _token_mixer(xn, past, prm, i):
    B, T, _ = xn.shape
    proj = xn @ prm['w_in'][i] + prm['b_in'][i]
    dq, dk, dv, fq, fk, fv, ff, rx, rg, gl = _split_proj(proj)
    dq = dq.reshape(B, T, N_DIFF_HEADS, 2 * DIFF_HEAD_DIM)
    dk = dk.reshape(B, T, N_DIFF_HEADS, 2 * DIFF_HEAD_DIM)
    dv = dv.reshape(B, T, N_DIFF_HEADS, DIFF_V_DIM)
    fq = fq.reshape(B, T, N_FOX_HEADS, FOX_HEAD_DIM)
    fk = fk.reshape(B, T, N_FOX_HEADS, FOX_HEAD_DIM)
    fv = fv.reshape(B, T, N_FOX_HEADS, FOX_HEAD_DIM)
    logf = jax.nn.log_sigmoid(ff.astype(jnp.float32))
    cum_new = jnp.cumsum(logf, axis=1)
    if past is None:
        P = 0
        dk_all, dv_all, fk_all, fv_all = dk, dv, fk, fv
        cum_q = cum_new
        cum_k = cum_new
        h0 = jnp.zeros((B, D_RNN), jnp.float32)
        conv_hist = jnp.zeros((B, CONV_WIDTH - 1, D_RNN), xn.dtype)
    else:
        pdk, pdv, pfk, pfv, plogf, h0, conv_hist = past
        P = pdk.shape[1]
        dk_all = jnp.concatenate([pdk.astype(dk.dtype), dk], axis=1)
        dv_all = jnp.concatenate([pdv.astype(dv.dtype), dv], axis=1)
        fk_all = jnp.concatenate([pfk.astype(fk.dtype), fk], axis=1)
        fv_all = jnp.concatenate([pfv.astype(fv.dtype), fv], axis=1)
        cum_past = jnp.cumsum(plogf.astype(jnp.float32), axis=1)
        cum_q = cum_past[:, -1:] + cum_new
        cum_k = jnp.concatenate([cum_past, cum_q], axis=1)
    q_pos = P + jnp.arange(T, dtype=jnp.int32)
    k_pos = jnp.arange(P + T, dtype=jnp.int32)

    lam_init = 0.8 - 0.6 * math.exp(-0.3 * i)
    lam = (jnp.exp(jnp.sum(prm['lam_q1'][i] * prm['lam_k1'][i]).astype(jnp.float32))
           - jnp.exp(jnp.sum(prm['lam_q2'][i] * prm['lam_k2'][i]).astype(jnp.float32)) + lam_init)
    ya = _diff_attention(dq, dk_all, dv_all, q_pos, k_pos, lam, lam_init, prm['diff_gain'][i])
    yb = _fox_attention(fq, fk_all, fv_all, cum_q, cum_k, q_pos, k_pos)
    lru, h_last, conv_new = _rglru(rx, conv_hist, h0, prm['conv_w'][i], prm['conv_b'][i], prm['w_a'][i],
                                   prm['b_a'][i], prm['w_i'][i], prm['b_i'][i], prm['lru_lambda'][i])
    yc = lru * jax.nn.gelu(rg)
    g = jax.nn.sigmoid(gl).reshape(B, T, N_BRANCHES, D_MODEL)
    merged = (g[:, :, 0] * (ya @ prm['w_pa'][i]) + g[:, :, 1] * (yb @ prm['w_pb'][i])
              + g[:, :, 2] * (yc @ prm['w_pc'][i]))
    out = merged @ prm['w_o'][i]
    return out, (dk, dv, fk, fv, logf, h_last, conv_new)


def _swiglu(x, w_g, w_u, w_d):
    return (jax.nn.silu(x @ w_g) * (x @ w_u)) @ w_d


def _moe(x, w_r, b_r, w_g, w_u, w_d):
    logits = (x @ w_r).astype(jnp.float32) + b_r.astype(jnp.float32)
    top_v, top_i = lax.top_k(logits, TOP_K)
    probs = jax.nn.softmax(top_v, axis=-1)
    combine = jnp.sum(jax.nn.one_hot(top_i, N_EXPERTS, dtype=jnp.float32) * probs[..., None], axis=-2)
    out = jnp.zeros_like(x)
    for e in range(N_EXPERTS):
        out = out + combine[..., e:e + 1].astype(x.dtype) * _swiglu(x, w_g[e], w_u[e], w_d[e])
    return out


def _trunk(x, p, past, prm):
    h = x
    states = []
    for i in range(DEPTH):
        layer_past = None if past is None else tuple(c[i] for c in past)
        mix, st = _token_mixer(_rmsnorm(h, prm['norm_mix'][i]), layer_past, prm, i)
        h = h + mix
        xn = _rmsnorm(h, prm['norm_ffn'][i])
        j = i // 2
        if i % 2 == 0:
            h = h + _swiglu(xn, prm['ffn_wg'][j], prm['ffn_wu'][j], prm['ffn_wd'][j])
        else:
            h = h + _moe(xn, prm['moe_wr'][j], prm['moe_br'][j], prm['moe_wg'][j], prm['moe_wu'][j], prm['moe_wd'][j])
        gate = jax.nn.sigmoid(_rmsnorm(h, prm['ple_norm'][i]) @ prm['ple_gate'][i])
        h = h + gate * (p[i] @ prm['ple_up'][i])
        states.append(st)
    y = _rmsnorm(h, prm['final_norm'])
    new_state = tuple(jnp.stack([st[k] for st in states]) for k in range(7))
    return y, new_state


def setup_inputs(seed: int = 0) -> dict:
    key = jax.random.key(seed)
    ks = iter(jax.random.split(key, 64))

    def nrm(shape, scale):
        return jax.random.normal(next(ks), shape, jnp.float32) * scale

    def gain(shape):
        return 1.0 + nrm(shape, 0.01)

    u = jax.random.uniform(next(ks), (DEPTH, D_RNN), jnp.float32, minval=0.9, maxval=0.999)
    a0 = u ** (1.0 / LRU_C)
    lru_lambda = jnp.log(a0) - jnp.log1p(-a0)
    b_in = nrm((DEPTH, N_IN), 0.02).at[:, FORGET_OFF:FORGET_OFF + N_FOX_HEADS].add(FORGET_BIAS_INIT)
    return {
        "x_prompt": nrm((BATCH, SEQ, D_MODEL), 1.0),
        "x_sample": nrm((DEC_BATCH, DEC_SEQ, D_MODEL), 1.0),
        "cache_diff_k": nrm((DEPTH, DEC_BATCH, PAST_LEN, N_DIFF_HEADS, 2 * DIFF_HEAD_DIM), 1.0),
        "cache_diff_v": nrm((DEPTH, DEC_BATCH, PAST_LEN, N_DIFF_HEADS, DIFF_V_DIM), 1.0),
        "cache_fox_k": nrm((DEPTH, DEC_BATCH, PAST_LEN, N_FOX_HEADS, FOX_HEAD_DIM), 1.0),
        "cache_fox_v": nrm((DEPTH, DEC_BATCH, PAST_LEN, N_FOX_HEADS, FOX_HEAD_DIM), 1.0),
        "cache_fox_logf": jax.nn.log_sigmoid(nrm((DEPTH, DEC_BATCH, PAST_LEN, N_FOX_HEADS), 1.0) + FORGET_BIAS_INIT),
        "state_rnn_h": nrm((DEPTH, DEC_BATCH, D_RNN), 0.5),
        "state_rnn_conv": nrm((DEPTH, DEC_BATCH, CONV_WIDTH - 1, D_RNN), 1.0),
        "p_prompt": nrm((DEPTH, BATCH, SEQ, D_PLE), 1.0),
        "p_sample": nrm((DEPTH, DEC_BATCH, DEC_SEQ, D_PLE), 1.0),
        "norm_mix": gain((DEPTH, D_MODEL)),
        "w_in": nrm((DEPTH, D_MODEL, N_IN), D_MODEL ** -0.5),
        "b_in": b_in,
        "lam_q1": nrm((DEPTH, DIFF_HEAD_DIM), 0.1),
        "lam_k1": nrm((DEPTH, DIFF_HEAD_DIM), 0.1),
        "lam_q2": nrm((DEPTH, DIFF_HEAD_DIM), 0.1),
        "lam_k2": nrm((DEPTH, DIFF_HEAD_DIM), 0.1),
        "diff_gain": gain((DEPTH, DIFF_V_DIM)),
        "conv_w": nrm((DEPTH, CONV_WIDTH, D_RNN), CONV_WIDTH ** -0.5),
        "conv_b": nrm((DEPTH, D_RNN), 0.02),
        "w_a": nrm((DEPTH, N_RNN_BLOCKS, RNN_BLOCK, RNN_BLOCK), RNN_BLOCK ** -0.5),
        "b_a": nrm((DEPTH, D_RNN), 0.02),
        "w_i": nrm((DEPTH, N_RNN_BLOCKS, RNN_BLOCK, RNN_BLOCK), RNN_BLOCK ** -0.5),
        "b_i": nrm((DEPTH, D_RNN), 0.02),
        "lru_lambda": lru_lambda,
        "w_pa": nrm((DEPTH, DIFF_V, D_MODEL), DIFF_V ** -0.5),
        "w_pb": nrm((DEPTH, FOX_W, D_MODEL), FOX_W ** -0.5),
        "w_pc": nrm((DEPTH, D_RNN, D_MODEL), D_RNN ** -0.5),
        "w_o": nrm((DEPTH, D_MODEL, D_MODEL), D_MODEL ** -0.5),
        "norm_ffn": gain((DEPTH, D_MODEL)),
        "ffn_wg": nrm((N_DENSE, D_MODEL, D_FF), D_MODEL ** -0.5),
        "ffn_wu": nrm((N_DENSE, D_MODEL, D_FF), D_MODEL ** -0.5),
        "ffn_wd": nrm((N_DENSE, D_FF, D_MODEL), D_FF ** -0.5),
        "moe_wr": nrm((N_MOE, D_MODEL, N_EXPERTS), D_MODEL ** -0.5),
        "moe_br": nrm((N_MOE, N_EXPERTS), 0.01),
        "moe_wg": nrm((N_MOE, N_EXPERTS, D_MODEL, D_FF_EXPERT), D_MODEL ** -0.5),
        "moe_wu": nrm((N_MOE, N_EXPERTS, D_MODEL, D_FF_EXPERT), D_MODEL ** -0.5),
        "moe_wd": nrm((N_MOE, N_EXPERTS, D_FF_EXPERT, D_MODEL), D_FF_EXPERT ** -0.5),
        "ple_norm": gain((DEPTH, D_MODEL)),
        "ple_up": nrm((DEPTH, D_PLE, D_MODEL), D_PLE ** -0.5),
        "ple_gate": nrm((DEPTH, D_MODEL, D_MODEL), D_MODEL ** -0.5),
        "final_norm": gain((D_MODEL,)),
    }


def reference(x_prompt, x_sample, cache_diff_k, cache_diff_v, cache_fox_k, cache_fox_v, cache_fox_logf,
              state_rnn_h, state_rnn_conv, p_prompt, p_sample, norm_mix, w_in, b_in, lam_q1, lam_k1,
              lam_q2, lam_k2, diff_gain, conv_w, conv_b, w_a, b_a, w_i, b_i, lru_lambda, w_pa, w_pb, w_pc,
              w_o, norm_ffn, ffn_wg, ffn_wu, ffn_wd, moe_wr, moe_br, moe_wg, moe_wu, moe_wd, ple_norm,
              ple_up, ple_gate, final_norm):
    prm = dict(norm_mix=norm_mix, w_in=w_in, b_in=b_in, lam_q1=lam_q1, lam_k1=lam_k1, lam_q2=lam_q2,
               lam_k2=lam_k2, diff_gain=diff_gain, conv_w=conv_w, conv_b=conv_b, w_a=w_a, b_a=b_a,
               w_i=w_i, b_i=b_i, lru_lambda=lru_lambda, w_pa=w_pa, w_pb=w_pb, w_pc=w_pc, w_o=w_o,
               norm_ffn=norm_ffn, ffn_wg=ffn_wg, ffn_wu=ffn_wu, ffn_wd=ffn_wd, moe_wr=moe_wr,
               moe_br=moe_br, moe_wg=moe_wg, moe_wu=moe_wu, moe_wd=moe_wd, ple_norm=ple_norm,
               ple_up=ple_up, ple_gate=ple_gate, final_norm=final_norm)
    y_prompt, sp = _trunk(x_prompt, p_prompt, None, prm)
    past = (cache_diff_k, cache_diff_v, cache_fox_k, cache_fox_v, cache_fox_logf, state_rnn_h, state_rnn_conv)
    y_sample, ss = _trunk(x_sample, p_sample, past, prm)
    p_dk, p_dv, p_fk, p_fv, p_lf, p_h, p_conv = sp
    s_dk, s_dv, s_fk, s_fv, s_lf, s_h, s_conv = ss
    return (y_prompt, y_sample, p_dk, p_dv, p_fk, p_fv, p_lf, p_h, p_conv,
            s_dk, s_dv, s_fk, s_fv, s_lf, s_h, s_conv)
```

```python
import functools
import math

import jax
import jax.numpy as jnp
from jax import lax
from jax.experimental import pallas as pl
from jax.experimental.pallas import tpu as pltpu

F32 = jnp.float32
BF = jnp.bfloat16

EPS = 1e-6
CHUNK = 64
CHUNK_SHIFT = 6
LRU_C = 8.0
FORGET_PAD = 128
NEG = -1e30
MIB = 1024 * 1024
V7X_VMEM_CAP = 60 * MIB
LANE = 128
RNN_CT = 256


def _pick(n, cands):
    for c in cands:
        if n % c == 0:
            return c
    raise ValueError(f"no tile in {cands} divides {n}")


def _cp(sem, vmem_bytes):
    return pltpu.CompilerParams(dimension_semantics=sem,
                                vmem_limit_bytes=int(min(max(vmem_bytes, 16 * MIB), V7X_VMEM_CAP)))


def _rms(xf, g):
    return xf * lax.rsqrt(jnp.mean(xf * xf, axis=-1, keepdims=True) + EPS) * g


def _rmsnorm_kernel(x_ref, g_ref, o_ref):
    o_ref[...] = _rms(x_ref[...], g_ref[...]).astype(o_ref.dtype)


def _rmsnorm(x, g, out_dtype):
    n, d = x.shape
    tm = _pick(n, (512, 256, 128, 64, 8))
    return pl.pallas_call(
        _rmsnorm_kernel,
        grid=(n // tm,),
        in_specs=[pl.BlockSpec((tm, d), lambda i: (i, 0)), pl.BlockSpec((1, d), lambda i: (0, 0))],
        out_specs=pl.BlockSpec((tm, d), lambda i: (i, 0)),
        out_shape=jax.ShapeDtypeStruct((n, d), out_dtype),
        compiler_params=_cp(("parallel",), 6 * tm * d * 4),
        name="rmsnorm",
    )(x, g.reshape(1, d))


def _mm_kernel(x_ref, w_ref, b_ref, s_ref, *o_refs, act):
    acc = jnp.dot(x_ref[...], w_ref[...], preferred_element_type=F32) + b_ref[...]
    if act == "sigmoid":
        acc = jax.nn.sigmoid(acc)
    elif act == "gelu":
        acc = jax.nn.gelu(acc)
    elif act == "log_sigmoid":
        acc = jax.nn.log_sigmoid(acc)
    elif act == "scale":
        acc = acc * s_ref[...]
    for o in o_refs:
        o[...] = acc.astype(o.dtype)


def _mm(x, w, b, out_dtypes, act=None, scale=None, seg=None):
    m, k = x.shape
    n = w.shape[1]
    tm = _pick(m, (1024, 512, 256, 128, 64, 8))
    tn = seg if seg is not None else _pick(n, (512, 256, 128))
    if scale is None:
        scale = jnp.ones((n,), F32)
    if seg is None:
        out_spec = pl.BlockSpec((tm, tn), lambda i, j: (i, j))
        out_shape = (m, n)
    else:
        out_spec = pl.BlockSpec((None, tm, tn), lambda i, j: (j, i, 0))
        out_shape = (n // seg, m, seg)
    outs = pl.pallas_call(
        functools.partial(_mm_kernel, act=act),
        grid=(m // tm, n // tn),
        in_specs=[pl.BlockSpec((tm, k), lambda i, j: (i, 0)),
                  pl.BlockSpec((k, tn), lambda i, j: (0, j)),
                  pl.BlockSpec((1, tn), lambda i, j: (0, j)),
                  pl.BlockSpec((1, tn), lambda i, j: (0, j))],
        out_specs=[out_spec] * len(out_dtypes),
        out_shape=[jax.ShapeDtypeStruct(out_shape, dt) for dt in out_dtypes],
        compiler_params=_cp(("parallel", "parallel"),
                            2 * (tm * k * 2 + k * tn * 2 + tm * tn * 6) + 3 * tm * tn * 4),
        name="mm_" + (act or "lin"),
    )(x, w, b.reshape(1, n).astype(F32), scale.reshape(1, n).astype(F32))
    return outs


def _osm_step(q, kb, vb, bias, mask, st):
    m, l, acc = st
    s = lax.dot_general(q, kb, (((1,), (1,)), ((), ())), preferred_element_type=F32) + bias
    if mask is not None:
        s = jnp.where(mask, s, NEG)
    m_new = jnp.maximum(m, jnp.max(s, axis=-1, keepdims=True))
    alpha = jnp.exp(m - m_new)
    p = jnp.exp(s - m_new)
    l = alpha * l + jnp.sum(p, axis=-1, keepdims=True)
    acc = alpha * acc + jnp.dot(p.astype(BF), vb, preferred_element_type=F32)
    return m_new, l, acc


def _osm_init(rows, dv):
    return (jnp.full((rows, 1), NEG, F32), jnp.zeros((rows, 1), F32), jnp.zeros((rows, dv), F32))


def _diff_split_q(q, hd):
    lane = lax.broadcasted_iota(jnp.int32, q.shape, 1)
    zero = jnp.zeros_like(q)
    return jnp.concatenate([jnp.where(lane < hd, q, zero), jnp.where(lane >= hd, q, zero)], axis=0)


def _diff_finish(st, lamp_ref, gain_ref, tq, lam_init):
    _, l, acc = st
    o = acc / l
    lp = lamp_ref[...]
    lam = (jnp.exp(jnp.sum(lp[0:1] * lp[1:2], axis=-1, keepdims=True))
           - jnp.exp(jnp.sum(lp[2:3] * lp[3:4], axis=-1, keepdims=True)) + lam_init)
    od = o[:tq] - lam * o[tq:]
    return _rms(od, gain_ref[...]) * (1.0 - lam_init)


def _diff_prompt_kernel(slopes_ref, lamp_ref, gain_ref, q_ref, k_ref, v_ref, o_ref, *, tq, hd, lam_init):
    h = pl.program_id(1)
    qi = pl.program_id(2)
    slope = slopes_ref[h]
    qq = _diff_split_q(q_ref[...], hd)
    r = lax.broadcasted_iota(jnp.int32, (tq, 1), 0)
    qpos = qi * tq + jnp.concatenate([r, r], axis=0)
    col = lax.broadcasted_iota(jnp.int32, (1, tq), 1)

    def blk(ki, st, masked):
        off = pl.multiple_of(ki * tq, tq)
        kpos = ki * tq + col
        bias = -slope * jnp.abs(qpos - kpos).astype(F32)
        mask = ((kpos >> CHUNK_SHIFT) <= (qpos >> CHUNK_SHIFT)) if masked else None
        return _osm_step(qq, k_ref[pl.ds(off, tq), :], v_ref[pl.ds(off, tq), :], bias, mask, st)

    st = lax.fori_loop(0, qi, lambda ki, st: blk(ki, st, False), _osm_init(2 * tq, v_ref.shape[-1]))
    st = blk(qi, st, True)
    o_ref[...] = _diff_finish(st, lamp_ref, gain_ref, tq, lam_init).astype(o_ref.dtype)


def _fox_prompt_kernel(q_ref, k_ref, v_ref, nck_ref, o_ref, *, tq):
    qi = pl.program_id(2)
    q = q_ref[...]
    qpos = qi * tq + lax.broadcasted_iota(jnp.int32, (tq, 1), 0)
    col = lax.broadcasted_iota(jnp.int32, (1, tq), 1)

    def blk(ki, st, masked):
        off = pl.multiple_of(ki * tq, tq)
        mask = ((ki * tq + col) <= qpos) if masked else None
        return _osm_step(q, k_ref[pl.ds(off, tq), :], v_ref[pl.ds(off, tq), :],
                         nck_ref[:, pl.ds(off, tq)], mask, st)

    st = lax.fori_loop(0, qi, lambda ki, st: blk(ki, st, False), _osm_init(tq, v_ref.shape[-1]))
    _, l, acc = blk(qi, st, True)
    o_ref[...] = (acc / l).astype(o_ref.dtype)


def _prompt_attention(kind, q, kv, b, t, nh, dh, q_col0, k_seg, v_seg, extra, lam_init=None):
    tq = _pick(t, (256, 128, 64))
    nq = t // tq
    q_spec = pl.BlockSpec((tq, dh), lambda bi, h, qi: (bi * nq + qi, q_col0 + h))
    k_spec = pl.BlockSpec((None, t, dh), lambda bi, h, qi: (k_seg, bi, h))
    v_spec = pl.BlockSpec((None, t, dh), lambda bi, h, qi: (v_seg, bi, h))
    o_spec = pl.BlockSpec((tq, dh), lambda bi, h, qi: (bi * nq + qi, h))
    vm = 2 * (2 * t * dh * 2 + 2 * tq * dh * 2) + 16 * tq * tq * 4 + 4 * MIB
    if kind == "diff":
        slopes, lamp, gain = extra
        kern = functools.partial(_diff_prompt_kernel, tq=tq, hd=dh // 2, lam_init=lam_init)
        in_specs = [pl.BlockSpec(memory_space=pltpu.SMEM),
                    pl.BlockSpec(lamp.shape, lambda bi, h, qi: (0, 0)),
                    pl.BlockSpec(gain.shape, lambda bi, h, qi: (0, 0)),
                    q_spec, k_spec, v_spec]
        args = (slopes, lamp, gain, q, kv, kv)
    else:
        (nck,) = extra
        kern = functools.partial(_fox_prompt_kernel, tq=tq)
        in_specs = [q_spec, k_spec, v_spec,
                    pl.BlockSpec((None, None, 1, t), lambda bi, h, qi: (bi, h, 0, 0))]
        args = (q, kv, kv, nck)
    return pl.pallas_call(
        kern,
        grid=(b, nh, nq),
        in_specs=in_specs,
        out_specs=o_spec,
        out_shape=jax.ShapeDtypeStruct((b * t, nh * dh), BF),
        compiler_params=_cp(("parallel", "parallel", "arbitrary"), vm),
        name=kind + "_prompt_attn",
    )(*args)


def _diff_sample_kernel(slopes_ref, lamp_ref, gain_ref, q_ref, kn_ref, vn_ref, kp_ref, vp_ref, o_ref,
                        *, tq, tk, past, hd, lam_init):
    h = pl.program_id(1)
    slope = slopes_ref[h]
    qq = _diff_split_q(q_ref[...], hd)
    r = lax.broadcasted_iota(jnp.int32, (tq, 1), 0)
    qpos = past + jnp.concatenate([r, r], axis=0)

    def bias_of(kpos):
        return -slope * jnp.abs(qpos - kpos).astype(F32)

    def blk(c, st):
        off = pl.multiple_of(c * tk, tk)
        kpos = c * tk + lax.broadcasted_iota(jnp.int32, (1, tk), 1)
        return _osm_step(qq, kp_ref[pl.ds(off, tk), :].astype(BF), vp_ref[pl.ds(off, tk), :].astype(BF),
                         bias_of(kpos), None, st)

    st = lax.fori_loop(0, past // tk, blk, _osm_init(2 * tq, vn_ref.shape[-1]))
    kpos = past + lax.broadcasted_iota(jnp.int32, (1, tq), 1)
    mask = (kpos >> CHUNK_SHIFT) <= (qpos >> CHUNK_SHIFT)
    st = _osm_step(qq, kn_ref[...], vn_ref[...], bias_of(kpos), mask, st)
    o_ref[...] = _diff_finish(st, lamp_ref, gain_ref, tq, lam_init).astype(o_ref.dtype)


def _fox_sample_kernel(q_ref, kn_ref, vn_ref, kp_ref, vp_ref, nckp_ref, nckn_ref, o_ref, *, tq, tk, past):
    q = q_ref[...]

    def blk(c, st):
        off = pl.multiple_of(c * tk, tk)
        return _osm_step(q, kp_ref[pl.ds(off, tk), :].astype(BF), vp_ref[pl.ds(off, tk), :].astype(BF),
                         nckp_ref[:, pl.ds(off, tk)], None, st)

    st = lax.fori_loop(0, past // tk, blk, _osm_init(tq, vn_ref.shape[-1]))
    mask = lax.broadcasted_iota(jnp.int32, (1, tq), 1) <= lax.broadcasted_iota(jnp.int32, (tq, 1), 0)
    _, l, acc = _osm_step(q, kn_ref[...], vn_ref[...], nckn_ref[...], mask, st)
    o_ref[...] = (acc / l).astype(o_ref.dtype)


def _sample_attention(kind, q, kv, cache_k, cache_v, layer, row0, b, t, nh, dh, q_col0, k_seg, v_seg, extra,
                      lam_init=None):
    past = cache_k.shape[2]
    tk = _pick(past, (512, 256, 128, 64))
    rb = row0 // t
    q_spec = pl.BlockSpec((t, dh), lambda bi, h: (rb + bi, q_col0 + h))
    kn_spec = pl.BlockSpec((None, t, dh), lambda bi, h: (k_seg, rb + bi, h))
    vn_spec = pl.BlockSpec((None, t, dh), lambda bi, h: (v_seg, rb + bi, h))
    kp_spec = pl.BlockSpec((None, None, past, dh), lambda bi, h: (layer, bi, 0, h))
    o_spec = pl.BlockSpec((t, dh), lambda bi, h: (bi, h))
    vm = 2 * (2 * past * dh * 4 + 4 * t * dh * 2) + 16 * t * tk * 4 + 8 * MIB
    if kind == "diff":
        slopes, lamp, gain = extra
        kern = functools.partial(_diff_sample_kernel, tq=t, tk=tk, past=past, hd=dh // 2, lam_init=lam_init)
        in_specs = [pl.BlockSpec(memory_space=pltpu.SMEM),
                    pl.BlockSpec(lamp.shape, lambda bi, h: (0, 0)),
                    pl.BlockSpec(gain.shape, lambda bi, h: (0, 0)),
                    q_spec, kn_spec, vn_spec, kp_spec, kp_spec]
        args = (slopes, lamp, gain, q, kv, kv, cache_k, cache_v)
    else:
        nckp, nckn = extra
        kern = functools.partial(_fox_sample_kernel, tq=t, tk=tk, past=past)
        in_specs = [q_spec, kn_spec, vn_spec, kp_spec, kp_spec,
                    pl.BlockSpec((None, None, 1, past), lambda bi, h: (bi, h, 0, 0)),
                    pl.BlockSpec((None, None, 1, t), lambda bi, h: (bi, h, 0, 0))]
        args = (q, kv, kv, cache_k, cache_v, nckp, nckn)
    return pl.pallas_call(
        kern,
        grid=(b, nh),
        in_specs=in_specs,
        out_specs=o_spec,
        out_shape=jax.ShapeDtypeStruct((b * t, nh * dh), BF),
        compiler_params=_cp(("parallel", "parallel"), vm),
        name=kind + "_sample_attn",
    )(*args)


def _expm1(x):
    u = jnp.exp(x)
    um1 = u - 1.0
    return jnp.where(um1 == 0.0, x, jnp.where(um1 == -1.0, -1.0, um1 * x / jnp.log(u)))


def _rglru_kernel(rx_ref, rg_ref, hist_ref, h0_ref, cw_ref, cb_ref, wa_ref, wi_ref, ba_ref, bi_ref, lam_ref,
                  yc_ref, hl_ref, cn_ref, prev_sc, h_sc, *, tt, cwidth):
    j = pl.program_id(2)

    @pl.when(j == 0)
    def _():
        prev_sc[...] = hist_ref[...]
        h_sc[...] = h0_ref[...]

    x = rx_ref[...]
    xp = jnp.concatenate([prev_sc[...], x], axis=0)
    cw = cw_ref[...]
    xc = cb_ref[...]
    for jj in range(cwidth):
        s0 = 8 - (cwidth - 1) + jj
        xc = xc + xp[s0:s0 + tt] * cw[jj:jj + 1]
    xcb = xc.astype(BF)
    r = jax.nn.sigmoid(jnp.dot(xcb, wa_ref[...], preferred_element_type=F32) + ba_ref[...])
    ig = jax.nn.sigmoid(jnp.dot(xcb, wi_ref[...], preferred_element_type=F32) + bi_ref[...])
    log_a = -LRU_C * r * jax.nn.softplus(-lam_ref[...])
    a = jnp.exp(log_a)
    bb = jnp.sqrt(-_expm1(2.0 * log_a)) * (ig * xc)

    row = lax.broadcasted_iota(jnp.int32, (tt, 1), 0)
    d = 1
    while d < tt:
        keep = row >= d
        a_sh = jnp.where(keep, pltpu.roll(a, d, 0), 1.0)
        b_sh = jnp.where(keep, pltpu.roll(bb, d, 0), 0.0)
        bb = a * b_sh + bb
        a = a * a_sh
        d *= 2
    hseq = bb + a * h_sc[...]

    yc_ref[...] = (hseq * rg_ref[...].astype(F32)).astype(yc_ref.dtype)
    h_sc[...] = hseq[tt - 1:tt]
    prev_sc[...] = xp[tt:tt + 8]

    @pl.when(j == pl.num_programs(2) - 1)
    def _():
        hl_ref[...] = hseq[tt - 1:tt]
        cn_ref[...] = xp[tt:tt + 8]


def _rglru(rx, rgg, hist8, h0, prm, row0, b, t):
    c = rx.shape[1]
    ct = RNN_CT if c % RNN_CT == 0 else c
    nc = c // ct
    tt = _pick(t, (256, 128, 64, 32, 16, 8))
    nt = t // tt
    rb = row0 // tt
    cw, cb, wa, wi, ba, bi, lam = prm
    cwidth = cw.shape[0]
    vec = lambda: pl.BlockSpec((1, ct), lambda bi_, ci, j: (0, ci))
    row_in = pl.BlockSpec((tt, ct), lambda bi_, ci, j: (rb + bi_ * nt + j, ci))
    return pl.pallas_call(
        functools.partial(_rglru_kernel, tt=tt, cwidth=cwidth),
        grid=(b, nc, nt),
        in_specs=[row_in, row_in,
                  pl.BlockSpec((None, 8, ct), lambda bi_, ci, j: (bi_, 0, ci)),
                  pl.BlockSpec((None, 1, ct), lambda bi_, ci, j: (bi_, 0, ci)),
                  pl.BlockSpec((cwidth, ct), lambda bi_, ci, j: (0, ci)),
                  vec(),
                  pl.BlockSpec((None, ct, ct), lambda bi_, ci, j: (ci, 0, 0)),
                  pl.BlockSpec((None, ct, ct), lambda bi_, ci, j: (ci, 0, 0)),
                  vec(), vec(), vec()],
        out_specs=[pl.BlockSpec((tt, ct), lambda bi_, ci, j: (bi_ * nt + j, ci)),
                   pl.BlockSpec((None, 1, ct), lambda bi_, ci, j: (bi_, 0, ci)),
                   pl.BlockSpec((None, 8, ct), lambda bi_, ci, j: (bi_, 0, ci))],
        out_shape=[jax.ShapeDtypeStruct((b * t, c), BF),
                   jax.ShapeDtypeStruct((b, 1, c), F32),
                   jax.ShapeDtypeStruct((b, 8, c), F32)],
        scratch_shapes=[pltpu.VMEM((8, ct), F32), pltpu.VMEM((1, ct), F32)],
        compiler_params=_cp(("parallel", "parallel", "arbitrary"), 32 * tt * ct * 4 + 8 * MIB),
        name="rglru",
    )(rx, rgg, hist8, h0, cw, cb.reshape(1, c), wa, wi, ba.reshape(1, c), bi.reshape(1, c), lam.reshape(1, c))


def _merge_kernel(ya_ref, yb_ref, yc_ref, g0_ref, g1_ref, g2_ref, wa_ref, wb_ref, wc_ref, o_ref):
    pa = jnp.dot(ya_ref[...], wa_ref[...], preferred_element_type=F32)
    pb = jnp.dot(yb_ref[...], wb_ref[...], preferred_element_type=F32)
    pc = jnp.dot(yc_ref[...], wc_ref[...], preferred_element_type=F32)
    o_ref[...] = (g0_ref[...].astype(F32) * pa + g1_ref[...].astype(F32) * pb
                  + g2_ref[...].astype(F32) * pc).astype(o_ref.dtype)


def _merge(ya, yb, yc, gates, wpa, wpb, wpc):
    n = ya.shape[0]
    d = wpa.shape[1]
    tm = _pick(n, (512, 256, 128, 64, 8))
    tn = _pick(d, (512, 256, 128))
    nd = d // tn
    act = lambda a: pl.BlockSpec((tm, a.shape[1]), lambda i, j: (i, 0))
    wsp = lambda w: pl.BlockSpec((w.shape[0], tn), lambda i, j: (0, j))
    gsp = lambda s: pl.BlockSpec((tm, tn), lambda i, j: (i, s * nd + j))
    return pl.pallas_call(
        _merge_kernel,
        grid=(n // tm, nd),
        in_specs=[act(ya), act(yb), act(yc), gsp(0), gsp(1), gsp(2), wsp(wpa), wsp(wpb), wsp(wpc)],
        out_specs=pl.BlockSpec((tm, tn), lambda i, j: (i, j)),
        out_shape=jax.ShapeDtypeStruct((n, d), BF),
        compiler_params=_cp(("parallel", "parallel"), 32 * MIB),
        name="merge",
    )(ya, yb, yc, gates, gates, gates, wpa, wpb, wpc)


def _mm_res_norm_kernel(x_ref, w_ref, h_ref, g_ref, ho_ref, xn_ref):
    hn = h_ref[...] + jnp.dot(x_ref[...], w_ref[...], preferred_element_type=F32)
    ho_ref[...] = hn
    xn_ref[...] = _rms(hn, g_ref[...]).astype(xn_ref.dtype)


def _mm_res_norm(x, w, h, g, xn_dtype):
    n, k = x.shape
    d = w.shape[1]
    tm = _pick(n, (256, 128, 64, 8))
    row = lambda c: pl.BlockSpec((tm, c), lambda i: (i, 0))
    return pl.pallas_call(
        _mm_res_norm_kernel,
        grid=(n // tm,),
        in_specs=[row(k), pl.BlockSpec((k, d), lambda i: (0, 0)), row(d), pl.BlockSpec((1, d), lambda i: (0, 0))],
        out_specs=[row(d), row(d)],
        out_shape=[jax.ShapeDtypeStruct((n, d), F32), jax.ShapeDtypeStruct((n, d), xn_dtype)],
        compiler_params=_cp(("parallel",), 2 * (k * d * 2 + tm * k * 2 + 3 * tm * d * 4) + 8 * MIB),
        name="out_proj",
    )(x, w, h, g.reshape(1, d))


def _ffn_kernel(x_ref, wg_ref, wu_ref, wd_ref, h_ref, g_ref, ho_ref, xn_ref, acc_sc):
    f = pl.program_id(1)

    @pl.when(f == 0)
    def _():
        acc_sc[...] = jnp.zeros_like(acc_sc)

    x = x_ref[...]
    a = (jax.nn.silu(jnp.dot(x, wg_ref[...], preferred_element_type=F32))
         * jnp.dot(x, wu_ref[...], preferred_element_type=F32)).astype(BF)
    acc_sc[...] += jnp.dot(a, wd_ref[...], preferred_element_type=F32)

    @pl.when(f == pl.num_programs(1) - 1)
    def _():
        hn = h_ref[...] + acc_sc[...]
        ho_ref[...] = hn
        xn_ref[...] = _rms(hn, g_ref[...]).astype(xn_ref.dtype)


def _ffn(x, wg, wu, wd, h, g):
    n, d = x.shape
    ff = wg.shape[1]
    tm = _pick(n, (512, 256, 128, 64, 8))
    tf = _pick(ff, (512, 256, 128))
    row = pl.BlockSpec((tm, d), lambda i, f: (i, 0))
    vm = 2 * (3 * d * tf * 2 + tm * d * (2 + 4 + 4 + 2)) + tm * d * 4 + 3 * tm * tf * 4 + 4 * MIB
    return pl.pallas_call(
        _ffn_kernel,
        grid=(n // tm, ff // tf),
        in_specs=[row,
                  pl.BlockSpec((d, tf), lambda i, f: (0, f)),
                  pl.BlockSpec((d, tf), lambda i, f: (0, f)),
                  pl.BlockSpec((tf, d), lambda i, f: (f, 0)),
                  row, pl.BlockSpec((1, d), lambda i, f: (0, 0))],
        out_specs=[row, row],
        out_shape=[jax.ShapeDtypeStruct((n, d), F32), jax.ShapeDtypeStruct((n, d), BF)],
        scratch_shapes=[pltpu.VMEM((tm, d), F32)],
        compiler_params=_cp(("parallel", "arbitrary"), vm),
        name="ffn",
    )(x, wg, wu, wd, h, g.reshape(1, d))


def _router_kernel(x_ref, wh_ref, wl_ref, b_ref, e_ref, p_ref, *, n_exp):
    x = x_ref[...]
    xh = x.astype(BF)
    xl = (x - xh.astype(F32)).astype(BF)
    wh = wh_ref[...]
    logits = (jnp.dot(xh, wh, preferred_element_type=F32) + jnp.dot(xl, wh, preferred_element_type=F32)
              + jnp.dot(xh, wl_ref[...], preferred_element_type=F32) + b_ref[...])
    lane = lax.broadcasted_iota(jnp.int32, logits.shape, 1)
    logits = jnp.where(lane < n_exp, logits, NEG)
    m1 = jnp.max(logits, axis=-1, keepdims=True)
    i1 = jnp.min(jnp.where(logits == m1, lane, LANE), axis=-1, keepdims=True)
    rest = jnp.where(lane == i1, NEG, logits)
    m2 = jnp.max(rest, axis=-1, keepdims=True)
    i2 = jnp.min(jnp.where(rest == m2, lane, LANE), axis=-1, keepdims=True)
    e2 = jnp.exp(m2 - m1)
    p1 = 1.0 / (1.0 + e2)
    p2 = e2 / (1.0 + e2)
    e_ref[...] = jnp.where(lane == 0, i1, jnp.where(lane == 1, i2, 0))
    p_ref[...] = jnp.where(lane == 0, p1, jnp.where(lane == 1, p2, 0.0))


def _router(x, wr, br):
    n, d = x.shape
    n_exp = wr.shape[1]
    tm = _pick(n, (256, 128, 64, 8))
    wpad = jnp.zeros((d, LANE), F32).at[:, :n_exp].set(wr)
    wh = wpad.astype(BF)
    wl = (wpad - wh.astype(F32)).astype(BF)
    bpad = jnp.zeros((1, LANE), F32).at[0, :n_exp].set(br)
    row = lambda: pl.BlockSpec((tm, LANE), lambda i: (i, 0))
    return pl.pallas_call(
        functools.partial(_router_kernel, n_exp=n_exp),
        grid=(n // tm,),
        in_specs=[pl.BlockSpec((tm, d), lambda i: (i, 0)),
                  pl.BlockSpec((d, LANE), lambda i: (0, 0)),
                  pl.BlockSpec((d, LANE), lambda i: (0, 0)),
                  pl.BlockSpec((1, LANE), lambda i: (0, 0))],
        out_specs=[row(), row()],
        out_shape=[jax.ShapeDtypeStruct((n, LANE), jnp.int32), jax.ShapeDtypeStruct((n, LANE), F32)],
        compiler_params=_cp(("parallel",), 16 * MIB),
        name="moe_router",
    )(x, wh, wl, bpad)


def _moe_ffn_kernel(te_ref, nu_ref, rt_ref, x_hbm, wg_ref, wu_ref, wd_ref, y_ref, xf_sc, xb_sc, acc_sc, sem,
                    *, tm):
    i = pl.program_id(0)
    f = pl.program_id(1)

    def row_copy(r, tok):
        return pltpu.make_async_copy(x_hbm.at[pl.ds(tok, 1)], xf_sc.at[pl.ds(r, 1)], sem.at[0])

    @pl.when(i < nu_ref[0])
    def _():
        @pl.when(f == 0)
        def _():
            def issue(r, c):
                row_copy(r, rt_ref[0, r]).start()
                return c

            def drain(r, c):
                row_copy(r, 0).wait()
                return c

            lax.fori_loop(0, tm, issue, 0)
            lax.fori_loop(0, tm, drain, 0)
            xb_sc[...] = xf_sc[...].astype(BF)
            acc_sc[...] = jnp.zeros_like(acc_sc)

        x = xb_sc[...]
        a = (jax.nn.silu(jnp.dot(x, wg_ref[...], preferred_element_type=F32))
             * jnp.dot(x, wu_ref[...], preferred_element_type=F32)).astype(BF)
        acc_sc[...] += jnp.dot(a, wd_ref[...], preferred_element_type=F32)

        @pl.when(f == pl.num_programs(1) - 1)
        def _():
            y_ref[...] = acc_sc[...]

    @pl.when(jnp.logical_and(i >= nu_ref[0], f == pl.num_programs(1) - 1))
    def _():
        y_ref[...] = jnp.zeros_like(y_ref)


def _moe_ffn(x, tile_expert, row_token, n_used, wg, wu, wd, tm):
    d = x.shape[1]
    ff = wg.shape[2]
    nt = tile_expert.shape[0]
    tf = _pick(ff, (256, 128))
    nf = ff // tf

    def tile(i, nu):
        return jnp.minimum(i, nu[0] - 1)

    def fidx(i, f, nu):
        return jnp.where(i < nu[0], f, nf - 1)

    grid_spec = pltpu.PrefetchScalarGridSpec(
        num_scalar_prefetch=2,
        grid=(nt, nf),
        in_specs=[pl.BlockSpec((None, 1, tm), lambda i, f, te, nu: (i, 0, 0), memory_space=pltpu.SMEM),
                  pl.BlockSpec(memory_space=pl.ANY),
                  pl.BlockSpec((None, d, tf), lambda i, f, te, nu: (te[tile(i, nu)], 0, fidx(i, f, nu))),
                  pl.BlockSpec((None, d, tf), lambda i, f, te, nu: (te[tile(i, nu)], 0, fidx(i, f, nu))),
                  pl.BlockSpec((None, tf, d), lambda i, f, te, nu: (te[tile(i, nu)], fidx(i, f, nu), 0))],
        out_specs=pl.BlockSpec((tm, d), lambda i, f, te, nu: (i, 0)),
        scratch_shapes=[pltpu.VMEM((tm, d), F32), pltpu.VMEM((tm, d), BF), pltpu.VMEM((tm, d), F32),
                        pltpu.SemaphoreType.DMA((1,))],
    )
    vm = 2 * (3 * d * tf * 2 + tm * d * 4) + tm * d * 10 + 3 * tm * tf * 4 + 4 * MIB
    return pl.pallas_call(
        functools.partial(_moe_ffn_kernel, tm=tm),
        grid_spec=grid_spec,
        out_shape=jax.ShapeDtypeStruct((nt * tm, d), F32),
        compiler_params=_cp(("arbitrary", "arbitrary"), vm),
        name="moe_ffn",
    )(tile_expert, n_used, row_token.reshape(nt, 1, tm), x, wg, wu, wd)


def _moe_combine_kernel(pos_ref, y_hbm, p_ref, h_ref, g_ref, ho_ref, xn_ref, y_sc, sem, *, tm):
    def row_copy(k, r, src):
        return pltpu.make_async_copy(y_hbm.at[pl.ds(src, 1)], y_sc.at[k, pl.ds(r, 1)], sem.at[0])

    def issue(r, c):
        row_copy(0, r, pos_ref[0, 2 * r]).start()
        row_copy(1, r, pos_ref[0, 2 * r + 1]).start()
        return c

    def drain(r, c):
        row_copy(0, r, 0).wait()
        row_copy(1, r, 0).wait()
        return c

    lax.fori_loop(0, tm, issue, 0)
    lax.fori_loop(0, tm, drain, 0)
    p = p_ref[...]
    hn = h_ref[...] + (p[:, 0:1] * y_sc[0] + p[:, 1:2] * y_sc[1])
    ho_ref[...] = hn
    xn_ref[...] = _rms(hn, g_ref[...]).astype(xn_ref.dtype)


def _moe_combine(pos, y, p, h, g):
    n, d = h.shape
    tm = _pick(n, (256, 128, 64, 8))
    row = lambda c: pl.BlockSpec((tm, c), lambda i: (i, 0))
    return pl.pallas_call(
        functools.partial(_moe_combine_kernel, tm=tm),
        grid=(n // tm,),
        in_specs=[pl.BlockSpec((None, 1, 2 * tm), lambda i: (i, 0, 0), memory_space=pltpu.SMEM),
                  pl.BlockSpec(memory_space=pl.ANY), row(LANE), row(d),
                  pl.BlockSpec((1, d), lambda i: (0, 0))],
        out_specs=[row(d), row(d)],
        scratch_shapes=[pltpu.VMEM((2, tm, d), F32), pltpu.SemaphoreType.DMA((1,))],
        out_shape=[jax.ShapeDtypeStruct((n, d), F32), jax.ShapeDtypeStruct((n, d), BF)],
        compiler_params=_cp(("arbitrary",), 2 * (tm * d * 10 + tm * LANE * 4) + 2 * tm * d * 4 + 8 * MIB),
        name="moe_combine",
    )(pos.reshape(n // tm, 1, 2 * tm), y, p, h, g.reshape(1, d))


def _moe(xn, h, wr, br, wg, wu, wd, g_next):
    n = xn.shape[0]
    n_exp = wr.shape[1]
    tm = _pick(n, (512, 256, 128, 64, 8))
    e_pad, p_pad = _router(xn, wr, br)
    top_e = e_pad[:, :2]
    onehot = (top_e[:, :, None] == jnp.arange(n_exp, dtype=jnp.int32)).astype(jnp.int32).sum(axis=1)
    counts = onehot.sum(axis=0)
    padded = ((counts + tm - 1) // tm) * tm
    ends = jnp.cumsum(padded)
    starts = ends - padded
    rank = jnp.cumsum(onehot, axis=0) - onehot
    pos = starts[top_e] + jnp.take_along_axis(rank, top_e, axis=1)
    nt = (2 * n + n_exp * (tm - 1) + tm - 1) // tm
    row_token = jnp.zeros((nt * tm,), jnp.int32).at[pos.reshape(-1)].set(
        jnp.repeat(jnp.arange(n, dtype=jnp.int32), 2))
    tile_expert = jnp.minimum(
        jnp.searchsorted(ends, jnp.arange(nt, dtype=jnp.int32) * tm, side="right"), n_exp - 1).astype(jnp.int32)
    n_used = (ends[-1:] // tm).astype(jnp.int32)
    y = _moe_ffn(xn, tile_expert, row_token, n_used, wg, wu, wd, tm)
    return _moe_combine(pos.reshape(-1).astype(jnp.int32), y, p_pad, h, g_next)


def _ple_kernel(x_ref, p_ref, wg_ref, wu_ref, h_ref, g_ref, *o_refs):
    gate = jax.nn.sigmoid(jnp.dot(x_ref[...], wg_ref[...], preferred_element_type=F32))
    up = jnp.dot(p_ref[...].astype(BF), wu_ref[...], preferred_element_type=F32)
    hn = h_ref[...] + gate * up
    xn = _rms(hn, g_ref[...])
    if len(o_refs) == 2:
        o_refs[0][...] = hn
    o_refs[-1][...] = xn.astype(o_refs[-1].dtype)


def _ple(x, p, wg, wu, h, g, xn_dtype, want_h):
    n, d = x.shape
    dp = p.shape[1]
    tm = _pick(n, (256, 128, 64, 8))
    row = lambda c: pl.BlockSpec((tm, c), lambda i: (i, 0))
    full = lambda a: pl.BlockSpec(a.shape, lambda i: (0, 0))
    out_specs = ([row(d)] if want_h else []) + [row(d)]
    out_shape = ([jax.ShapeDtypeStruct((n, d), F32)] if want_h else []) + [jax.ShapeDtypeStruct((n, d), xn_dtype)]
    outs = pl.pallas_call(
        _ple_kernel,
        grid=(n // tm,),
        in_specs=[row(d), row(dp), full(wg), full(wu), row(d), pl.BlockSpec((1, d), lambda i: (0, 0))],
        out_specs=out_specs,
        out_shape=out_shape,
        compiler_params=_cp(("parallel",), 2 * (d * d * 2 + dp * d * 2 + tm * d * 14 + tm * dp * 4) + 8 * MIB),
        name="ple",
    )(x, p, wg, wu, h, g.reshape(1, d))
    return (outs[0], outs[1]) if want_h else (None, outs[0])


def _block_diag(w, group):
    nb, r, _ = w.shape
    eye = jnp.eye(group, dtype=w.dtype)
    return jnp.einsum("ab,caij->caibj", eye, w.reshape(nb // group, group, r, r)).reshape(
        nb // group, group * r, group * r)


def kernel(x_prompt, x_sample, cache_diff_k, cache_diff_v, cache_fox_k, cache_fox_v, cache_fox_logf, state_rnn_h, state_rnn_conv, p_prompt, p_sample, norm_mix, w_in, b_in, lam_q1, lam_k1, lam_q2, lam_k2, diff_gain, conv_w, conv_b, w_a, b_a, w_i, b_i, lru_lambda, w_pa, w_pb, w_pc, w_o, norm_ffn, ffn_wg, ffn_wu, ffn_wd, moe_wr, moe_br, moe_wg, moe_wu, moe_wd, ple_norm, ple_up, ple_gate, final_norm):
    bp, tp, d = x_prompt.shape
    bs, ts, _ = x_sample.shape
    depth = w_in.shape[0]
    past = cache_diff_k.shape[2]
    hd_n, hd_w = cache_diff_k.shape[3], cache_diff_k.shape[4]
    hf_n, hf_w = cache_fox_k.shape[3], cache_fox_k.shape[4]
    dqk, dvv, fw = hd_n * hd_w, hd_n * cache_diff_v.shape[4], hf_n * hf_w
    c_rnn = state_rnn_h.shape[2]
    cwidth = conv_w.shape[1]
    assert hd_w == cache_diff_v.shape[4] == hf_w == LANE and dqk == dvv == fw
    assert past % CHUNK == 0 and ts <= CHUNK and cwidth - 1 <= 8
    np_, ns_ = bp * tp, bs * ts
    n = np_ + ns_
    assert np_ % ts == 0

    o_dq, o_dk, o_dv = 0, dqk, 2 * dqk
    o_fq = 2 * dqk + dvv
    o_fk, o_fv, o_ff = o_fq + fw, o_fq + 2 * fw, o_fq + 3 * fw
    o_rx = o_ff + hf_n
    o_rg = o_rx + c_rnn
    o_gl = o_rg + c_rnn

    def cols(a, lo, width):
        return lax.slice_in_dim(a, lo, lo + width, axis=a.ndim - 1)

    slopes = jnp.asarray([2.0 ** (-8.0 * (k + 1) / hd_n) for k in range(hd_n)], F32)
    q_scale = jnp.concatenate([jnp.full((dqk,), (hd_w // 2) ** -0.5, F32), jnp.full((fw,), hf_w ** -0.5, F32)])

    ck4 = cache_diff_k.reshape(depth, bs, past, dqk)
    cv4 = cache_diff_v.reshape(depth, bs, past, dvv)
    fk4 = cache_fox_k.reshape(depth, bs, past, fw)
    fv4 = cache_fox_v.reshape(depth, bs, past, fw)
    hist8_s = jnp.pad(state_rnn_conv, ((0, 0), (0, 0), (8 - (cwidth - 1), 0), (0, 0)))
    hist8_p = jnp.zeros((bp, 8, c_rnn), F32)
    h0_p = jnp.zeros((bp, 1, c_rnn), F32)

    h = jnp.concatenate([x_prompt.reshape(np_, d), x_sample.reshape(ns_, d)], axis=0)
    xn = _rmsnorm(h, norm_mix[0], BF)
    st_p, st_s = [], []
    y = None
    for i in range(depth):
        wi_l, bi_l = w_in[i], b_in[i]
        w_q = jnp.concatenate([cols(wi_l, o_dq, dqk), cols(wi_l, o_fq, fw)], axis=1).astype(BF)
        b_q = jnp.concatenate([cols(bi_l, o_dq, dqk), cols(bi_l, o_fq, fw)])
        (q16,) = _mm(xn, w_q, b_q, (BF,), act="scale", scale=q_scale)
        w_kv = jnp.concatenate([cols(wi_l, o_dk, dqk), cols(wi_l, o_dv, dvv), cols(wi_l, o_fk, fw),
                                cols(wi_l, o_fv, fw)], axis=1).astype(BF)
        b_kv = jnp.concatenate([cols(bi_l, o_dk, dqk), cols(bi_l, o_dv, dvv), cols(bi_l, o_fk, fw),
                                cols(bi_l, o_fv, fw)])
        kv32, kv16 = _mm(xn, w_kv, b_kv, (F32, BF), seg=dqk)
        w_ff = jnp.zeros((d, FORGET_PAD), F32).at[:, :hf_n].set(cols(wi_l, o_ff, hf_n)).astype(BF)
        b_ff = jnp.zeros((FORGET_PAD,), F32).at[:hf_n].set(cols(bi_l, o_ff, hf_n))
        (logf_pad,) = _mm(xn, w_ff, b_ff, (F32,), act="log_sigmoid")
        logf = logf_pad[:, :hf_n]
        (rx,) = _mm(xn, cols(wi_l, o_rx, c_rnn).astype(BF), cols(bi_l, o_rx, c_rnn), (F32,))
        (rgg,) = _mm(xn, cols(wi_l, o_rg, c_rnn).astype(BF), cols(bi_l, o_rg, c_rnn), (BF,), act="gelu")
        (gates,) = _mm(xn, cols(wi_l, o_gl, 3 * d).astype(BF), cols(bi_l, o_gl, 3 * d), (BF,), act="sigmoid")

        lam_init = 0.8 - 0.6 * math.exp(-0.3 * i)
        lamp = jnp.stack([lam_q1[i], lam_k1[i], lam_q2[i], lam_k2[i]])
        dextra = (slopes, lamp, diff_gain[i].reshape(1, -1))
        ya_p = _prompt_attention("diff", q16, kv16, bp, tp, hd_n, hd_w, 0, 0, 1, dextra, lam_init)
        ya_s = _sample_attention("diff", q16, kv16, ck4, cv4, i, np_, bs, ts, hd_n, hd_w, 0, 0, 1, dextra, lam_init)
        logf_p = logf[:np_].reshape(bp, tp, hf_n)
        logf_s = logf[np_:].reshape(bs, ts, hf_n)
        cum_p = jnp.cumsum(logf_p, axis=1)
        cum_past = jnp.cumsum(cache_fox_logf[i].astype(F32), axis=1)
        cum_s = cum_past[:, -1:] + jnp.cumsum(logf_s, axis=1)
        as_rows = lambda c: -jnp.transpose(c, (0, 2, 1))[:, :, None, :]
        yb_p = _prompt_attention("fox", q16, kv16, bp, tp, hf_n, hf_w, hd_n, 2, 3, (as_rows(cum_p),))
        yb_s = _sample_attention("fox", q16, kv16, fk4, fv4, i, np_, bs, ts, hf_n, hf_w, hd_n, 2, 3,
                                 (as_rows(cum_past), as_rows(cum_s)))
        grp = RNN_CT // w_a.shape[2] if c_rnn % RNN_CT == 0 else w_a.shape[1]
        rprm = (conv_w[i], conv_b[i], _block_diag(w_a[i], grp).astype(BF), _block_diag(w_i[i], grp).astype(BF),
                b_a[i], b_i[i], lru_lambda[i])
        yc_p, hl_p, cn_p = _rglru(rx, rgg, hist8_p, h0_p, rprm, 0, bp, tp)
        yc_s, hl_s, cn_s = _rglru(rx, rgg, hist8_s[i], state_rnn_h[i][:, None, :], rprm, np_, bs, ts)
        cat = lambda a, b_: jnp.concatenate([a, b_], axis=0)
        merged = _merge(cat(ya_p, ya_s), cat(yb_p, yb_s), cat(yc_p, yc_s), gates,
                        w_pa[i].astype(BF), w_pb[i].astype(BF), w_pc[i].astype(BF))
        j = i // 2
        is_moe = i % 2 == 1
        h, xn2 = _mm_res_norm(merged, w_o[i].astype(BF), h, norm_ffn[i], F32 if is_moe else BF)
        if is_moe:
            h, xn3 = _moe(xn2, h, moe_wr[j], moe_br[j], moe_wg[j].astype(BF), moe_wu[j].astype(BF),
                          moe_wd[j].astype(BF), ple_norm[i])
        else:
            h, xn3 = _ffn(xn2, ffn_wg[j].astype(BF), ffn_wu[j].astype(BF), ffn_wd[j].astype(BF), h, ple_norm[i])
        p_cat = jnp.concatenate([p_prompt[i].reshape(np_, -1), p_sample[i].reshape(ns_, -1)], axis=0)
        last = i == depth - 1
        g_next = final_norm if last else norm_mix[i + 1]
        h, xn = _ple(xn3, p_cat, ple_gate[i].astype(BF), ple_up[i].astype(BF), h, g_next,
                     F32 if last else BF, not last)
        if last:
            y = xn
        w0 = 8 - (cwidth - 1)
        st_p.append((kv32[0, :np_], kv32[1, :np_], kv32[2, :np_], kv32[3, :np_], logf_p, hl_p[:, 0], cn_p[:, w0:]))
        st_s.append((kv32[0, np_:], kv32[1, np_:], kv32[2, np_:], kv32[3, np_:], logf_s, hl_s[:, 0], cn_s[:, w0:]))

    def stack(sts, b, t):
        k = lambda idx: jnp.stack([s[idx] for s in sts])
        return (k(0).reshape(depth, b, t, hd_n, hd_w), k(1).reshape(depth, b, t, hd_n, hd_w),
                k(2).reshape(depth, b, t, hf_n, hf_w), k(3).reshape(depth, b, t, hf_n, hf_w),
                k(4), k(5), k(6))

    return (y[:np_].reshape(bp, tp, d), y[np_:].reshape(bs, ts, d)) + stack(st_p, bp, tp) + stack(st_s, bs, ts)
```

```python
import functools
import math

import jax
import jax.numpy as jnp
from jax import lax
from jax.experimental import pallas as pl
from jax.experimental.pallas import tpu as pltpu

F32 = jnp.float32
BF = jnp.bfloat16

EPS = 1e-6
CHUNK_SHIFT = 6
LRU_C = 8.0
NEG = -1e30
LOG2E = 1.4426950408889634
MIB = 1024 * 1024
V7X_VMEM_CAP = 60 * MIB
LANE = 128
RNN_CT = 256
ATT_RG = 256
ATT_RC = 32


def _pick(n, cands):
    for c in cands:
        if n % c == 0:
            return c
    raise ValueError(f"no tile in {cands} divides {n}")


def _cp(sem, vmem_bytes):
    return pltpu.CompilerParams(dimension_semantics=sem,
                                vmem_limit_bytes=int(min(max(vmem_bytes, 16 * MIB), V7X_VMEM_CAP)))


def _rms(xf, g):
    return xf * lax.rsqrt(jnp.mean(xf * xf, axis=-1, keepdims=True) + EPS) * g


def _rmsnorm_kernel(x_ref, g_ref, o_ref):
    o_ref[...] = _rms(x_ref[...], g_ref[...]).astype(o_ref.dtype)


def _rmsnorm(x, g, out_dtype):
    n, d = x.shape
    tm = _pick(n, (512, 256, 128, 64, 8))
    return pl.pallas_call(
        _rmsnorm_kernel,
        grid=(n // tm,),
        in_specs=[pl.BlockSpec((tm, d), lambda i: (i, 0)), pl.BlockSpec((1, d), lambda i: (0, 0))],
        out_specs=pl.BlockSpec((tm, d), lambda i: (i, 0)),
        out_shape=jax.ShapeDtypeStruct((n, d), out_dtype),
        compiler_params=_cp(("parallel",), 6 * tm * d * 4),
        name="rmsnorm",
    )(x, g.reshape(1, d))


def _mm_kernel(x_ref, w_ref, b_ref, s_ref, o_ref, *, act):
    acc = jnp.dot(x_ref[...], w_ref[...], preferred_element_type=F32) + b_ref[...]
    if act == "sigmoid":
        acc = jax.nn.sigmoid(acc)
    elif act == "gelu":
        acc = jax.nn.gelu(acc)
    elif act == "log_sigmoid":
        acc = jax.nn.log_sigmoid(acc)
    elif act == "scale":
        acc = acc * s_ref[...]
    o_ref[...] = acc.astype(o_ref.dtype)


def _mm(x, w, layer, col0, stride, b, out_dtype, act=None, scale=None):
    m, k = x.shape
    n = b.shape[0]
    tm = _pick(m, (1024, 512, 256, 128, 64, 8))
    tn = _pick(n, (512, 256, 128))
    if scale is None:
        scale = jnp.ones((n,), F32)
    return pl.pallas_call(
        functools.partial(_mm_kernel, act=act),
        grid=(m // tm, n // tn),
        in_specs=[pl.BlockSpec((tm, k), lambda i, j: (i, 0)),
                  pl.BlockSpec((None, k, tn), lambda i, j: (layer, 0, col0 + stride * j)),
                  pl.BlockSpec((1, tn), lambda i, j: (0, j)),
                  pl.BlockSpec((1, tn), lambda i, j: (0, j))],
        out_specs=pl.BlockSpec((tm, tn), lambda i, j: (i, j)),
        out_shape=jax.ShapeDtypeStruct((m, n), out_dtype),
        compiler_params=_cp(("parallel", "parallel"),
                            2 * (tm * k * 2 + k * tn * 2 + tm * tn * 4) + 3 * tm * tn * 4),
        name="mm_" + (act or "lin"),
    )(x, w, b.reshape(1, n).astype(F32), scale.reshape(1, n).astype(F32))


def _kv_kernel(x_ref, w_ref, b_ref, *refs, nseg, npt, nh, dh):
    outs = refs[2 * nseg:]
    kv16_ref, state_refs = outs[0], outs[1:]
    i = pl.program_id(0)
    j = pl.program_id(1)
    acc = jnp.dot(x_ref[...], w_ref[...], preferred_element_type=F32) + b_ref[...]
    kv16_ref[...] = acc.astype(kv16_ref.dtype)

    def put(o_ref):
        for h in range(nh):
            o_ref[:, h, :] = acc[:, h * dh:(h + 1) * dh]

    for s in range(nseg):
        pl.when(jnp.logical_and(j == s, i < npt))(functools.partial(put, state_refs[s]))
        pl.when(jnp.logical_and(j == s, i >= npt))(functools.partial(put, state_refs[nseg + s]))


def _kv_proj(x, w, layer, b, bufs_p, bufs_s, np_rows, nh, dh):
    m, k = x.shape
    seg = nh * dh
    nseg = len(bufs_p)
    tm = _pick(math.gcd(np_rows, m - np_rows), (512, 256, 128, 64, 8))
    npt = np_rows // tm
    any_spec = pl.BlockSpec(memory_space=pl.ANY)
    p_spec = pl.BlockSpec((None, tm, nh, dh), lambda i, j: (layer, jnp.minimum(i, npt - 1), 0, 0))
    s_spec = pl.BlockSpec((None, tm, nh, dh), lambda i, j: (layer, jnp.maximum(i - npt, 0), 0, 0))
    bufs = tuple(bufs_p) + tuple(bufs_s)
    outs = pl.pallas_call(
        functools.partial(_kv_kernel, nseg=nseg, npt=npt, nh=nh, dh=dh),
        grid=(m // tm, nseg),
        in_specs=[pl.BlockSpec((tm, k), lambda i, j: (i, 0)),
                  pl.BlockSpec((None, k, seg), lambda i, j: (layer, 0, j + 1 + j // 2)),
                  pl.BlockSpec((1, seg), lambda i, j: (0, j))] + [any_spec] * (2 * nseg),
        out_specs=[pl.BlockSpec((None, tm, seg), lambda i, j: (j, i, 0))] + [p_spec] * nseg + [s_spec] * nseg,
        out_shape=[jax.ShapeDtypeStruct((nseg, m, seg), BF)]
                  + [jax.ShapeDtypeStruct(a.shape, a.dtype) for a in bufs],
        input_output_aliases={3 + t: 1 + t for t in range(2 * nseg)},
        compiler_params=_cp(("arbitrary", "arbitrary"),
                            2 * (tm * k * 2 + k * seg * 2 + tm * seg * 2 + 2 * nseg * tm * seg * 8) + 4 * MIB),
        name="kv_proj",
    )(x, w, b.reshape(1, nseg * seg).astype(F32), *bufs)
    return outs[0], outs[1:1 + nseg], outs[1 + nseg:]


def _attn_scratch(rows, tk, dv, nh=None):
    per_head = (lambda s: s) if nh is None else (lambda s: (nh,) + s)
    return [pltpu.VMEM((rows, tk), F32), pltpu.VMEM((rows, tk), BF), pltpu.VMEM(per_head((rows, LANE)), F32),
            pltpu.VMEM((rows, LANE), F32), pltpu.VMEM(per_head((rows, dv + LANE)), F32)]


def _softmax_block(q, kb, vb, bias_fn, mask_fn, st):
    s_sc, p_sc, m_sc, a_sc, acc_sc = st
    rows, dv = q.shape[0], vb.shape[1]
    tk = kb.shape[0]
    rg = min(ATT_RG, rows)
    rc = min(ATT_RC, rg)
    e0 = jnp.where(lax.broadcasted_iota(jnp.int32, (tk, LANE), 1) == 0, 1.0, 0.0).astype(BF)
    vext = jnp.concatenate([vb, e0], axis=1)

    def scores(sl, r0, j):
        x = s_sc[sl, j * LANE:(j + 1) * LANE] + bias_fn(r0, j)
        return x if mask_fn is None else jnp.where(mask_fn(r0, j), x, NEG)

    for g in range(rows // rg):
        gs = slice(g * rg, (g + 1) * rg)
        s_sc[gs, :tk] = lax.dot_general(q[gs], kb, (((1,), (1,)), ((), ())), preferred_element_type=F32)
        for c in range(rg // rc):
            r0 = g * rg + c * rc
            sl = slice(r0, r0 + rc)
            mv = scores(sl, r0, 0)
            for j in range(1, tk // LANE):
                mv = jnp.maximum(mv, scores(sl, r0, j))
            m_old = m_sc[sl]
            m_new = jnp.maximum(m_old, jnp.broadcast_to(jnp.max(mv, axis=-1, keepdims=True), (rc, LANE)))
            a_sc[sl] = jnp.exp2(m_old - m_new)
            m_sc[sl] = m_new
        for c in range(rg // rc):
            r0 = g * rg + c * rc
            sl = slice(r0, r0 + rc)
            m_new = m_sc[sl]
            for j in range(tk // LANE):
                p_sc[sl, j * LANE:(j + 1) * LANE] = jnp.exp2(scores(sl, r0, j) - m_new).astype(BF)
        pv = jnp.dot(p_sc[gs, :tk], vext, preferred_element_type=F32)
        alpha = a_sc[gs]
        acc_sc[gs, :dv] = alpha * acc_sc[gs, :dv] + pv[:, :dv]
        acc_sc[gs, dv:] = alpha * acc_sc[gs, dv:] + pv[:, dv:]


def _state_init(st):
    _, _, m_sc, _, acc_sc = st
    m_sc[...] = jnp.full(m_sc.shape, NEG, F32)
    acc_sc[...] = jnp.zeros(acc_sc.shape, F32)


def _attn_out(acc, dv):
    return acc[:, :dv] / acc[:, dv:dv + 1]


def _diff_split_q(q, hd):
    lane = lax.broadcasted_iota(jnp.int32, q.shape, 1)
    zero = jnp.zeros_like(q)
    return jnp.concatenate([jnp.where(lane < hd, q, zero), jnp.where(lane >= hd, q, zero)], axis=0)


def _diff_bias_diag(slope2, qpos, kpos):
    bias = slope2 * (qpos - jnp.abs(qpos - kpos)).astype(F32)
    mask = (kpos >> CHUNK_SHIFT) <= (qpos >> CHUNK_SHIFT)
    return bias, mask


def _diff_finish(acc, lamp_ref, gain_ref, tq, lam_init):
    o = _attn_out(acc, gain_ref.shape[-1])
    lp = lamp_ref[...]
    lam = (jnp.exp(jnp.sum(lp[0:1] * lp[1:2], axis=-1, keepdims=True))
           - jnp.exp(jnp.sum(lp[2:3] * lp[3:4], axis=-1, keepdims=True)) + lam_init)
    od = o[:tq] - lam * o[tq:]
    return _rms(od, gain_ref[...]) * (1.0 - lam_init)


def _lane_tile(row, j):
    return row[:, j * LANE:(j + 1) * LANE]


def _chunk_rows(first):
    return first + lax.broadcasted_iota(jnp.int32, (ATT_RC, 1), 0)


def _diff_prompt_kernel(slopes_ref, lamp_ref, gain_ref, q_ref, k_ref, v_ref, o_ref, *st, tq, hd, lam_init):
    h = pl.program_id(1)
    qi = pl.program_id(2)
    slope2 = slopes_ref[h]
    qq = _diff_split_q(q_ref[...], hd)
    _state_init(st)
    col = lax.broadcasted_iota(jnp.int32, (1, tq), 1)

    def past_block(ki, c):
        off = pl.multiple_of(ki * tq, tq)
        bias = slope2 * (ki * tq + col).astype(F32)
        _softmax_block(qq, k_ref[pl.ds(off, tq), :], v_ref[pl.ds(off, tq), :],
                       lambda r0, j: _lane_tile(bias, j), None, st)
        return c

    lax.fori_loop(0, qi, past_block, 0)
    off = pl.multiple_of(qi * tq, tq)
    kpos = qi * tq + col

    def diag(r0, j):
        return _diff_bias_diag(slope2, _chunk_rows(qi * tq + r0 % tq), _lane_tile(kpos, j))

    _softmax_block(qq, k_ref[pl.ds(off, tq), :], v_ref[pl.ds(off, tq), :], lambda r0, j: diag(r0, j)[0],
                   lambda r0, j: diag(r0, j)[1], st)
    o_ref[...] = _diff_finish(st[4][...], lamp_ref, gain_ref, tq, lam_init).astype(o_ref.dtype)


def _fox_prompt_kernel(q_ref, k_ref, v_ref, nck_ref, o_ref, *st, tq):
    qi = pl.program_id(2)
    q = q_ref[...]
    _state_init(st)
    col = lax.broadcasted_iota(jnp.int32, (1, tq), 1)

    def block(ki, mask_fn):
        off = pl.multiple_of(ki * tq, tq)
        bias = nck_ref[:, pl.ds(off, tq)]
        _softmax_block(q, k_ref[pl.ds(off, tq), :], v_ref[pl.ds(off, tq), :],
                       lambda r0, j: _lane_tile(bias, j), mask_fn, st)

    def past_block(ki, c):
        block(ki, None)
        return c

    lax.fori_loop(0, qi, past_block, 0)
    block(qi, lambda r0, j: _lane_tile(col, j) <= _chunk_rows(r0))
    o_ref[...] = _attn_out(st[4][...], v_ref.shape[-1]).astype(o_ref.dtype)


def _prompt_attention(kind, q, kv, b, t, nh, dh, q_col0, k_seg, v_seg, extra, lam_init=None):
    tq = _pick(t, (512, 256, 128))
    nq = t // tq
    q_spec = pl.BlockSpec((tq, dh), lambda bi, h, qi: (bi * nq + qi, q_col0 + h))
    k_spec = pl.BlockSpec((None, t, dh), lambda bi, h, qi: (k_seg, bi, h))
    v_spec = pl.BlockSpec((None, t, dh), lambda bi, h, qi: (v_seg, bi, h))
    o_spec = pl.BlockSpec((tq, dh), lambda bi, h, qi: (bi * nq + qi, h))
    rows = 2 * tq if kind == "diff" else tq
    scratch = _attn_scratch(rows, tq, dh)
    vm = 2 * (2 * t * dh * 2 + 2 * tq * dh * 2) + rows * (6 * tq + 4 * LANE * 4) + 8 * ATT_RG * tq * 4 + 4 * MIB
    if kind == "diff":
        slopes, lamp, gain = extra
        kern = functools.partial(_diff_prompt_kernel, tq=tq, hd=dh // 2, lam_init=lam_init)
        in_specs = [pl.BlockSpec(memory_space=pltpu.SMEM),
                    pl.BlockSpec(lamp.shape, lambda bi, h, qi: (0, 0)),
                    pl.BlockSpec(gain.shape, lambda bi, h, qi: (0, 0)),
                    q_spec, k_spec, v_spec]
        args = (slopes, lamp, gain, q, kv, kv)
    else:
        (nck,) = extra
        kern = functools.partial(_fox_prompt_kernel, tq=tq)
        in_specs = [q_spec, k_spec, v_spec,
                    pl.BlockSpec((None, None, 1, t), lambda bi, h, qi: (bi, h, 0, 0))]
        args = (q, kv, kv, nck)
    return pl.pallas_call(
        kern,
        grid=(b, nh, nq),
        in_specs=in_specs,
        out_specs=o_spec,
        out_shape=jax.ShapeDtypeStruct((b * t, nh * dh), BF),
        scratch_shapes=scratch,
        compiler_params=_cp(("parallel", "parallel", "arbitrary"), vm),
        name=kind + "_prompt_attn",
    )(*args)


def _head_state(st, h):
    s_sc, p_sc, m_sc, a_sc, acc_sc = st
    return s_sc, p_sc, m_sc.at[h], a_sc, acc_sc.at[h]


def _head_rows(c_ref, h, tk, nh):
    return c_ref[pl.ds(h, tk, stride=nh), :].astype(BF)


def _pad_keys(x, tkn):
    return x if x.shape[0] == tkn else jnp.concatenate([x, jnp.zeros((tkn - x.shape[0], x.shape[1]), x.dtype)], 0)


def _diff_sample_kernel(slopes_ref, lamp_ref, gain_ref, q_ref, kn_ref, vn_ref, kp_ref, vp_ref, o_ref, *st,
                        tq, tk, tkn, past, nh, dh, lam_init):
    kb_i = pl.program_id(1)

    @pl.when(kb_i == 0)
    def _():
        _state_init(st)

    q = q_ref[...]
    qqs = [_diff_split_q(q[:, h * dh:(h + 1) * dh], dh // 2) for h in range(nh)]
    kposf = (kb_i * tk + lax.broadcasted_iota(jnp.int32, (1, tk), 1)).astype(F32)
    for h in range(nh):
        bias = slopes_ref[h] * kposf
        _softmax_block(qqs[h], _head_rows(kp_ref, h, tk, nh), _head_rows(vp_ref, h, tk, nh),
                       lambda r0, j: _lane_tile(bias, j), None, _head_state(st, h))

    @pl.when(kb_i == pl.num_programs(1) - 1)
    def _():
        col = lax.broadcasted_iota(jnp.int32, (1, tkn), 1)
        for h in range(nh):
            hs = slice(h * dh, (h + 1) * dh)

            def diag(r0, j):
                cj = _lane_tile(col, j)
                bias, mask = _diff_bias_diag(slopes_ref[h], _chunk_rows(past + r0 % tq), past + cj)
                return bias, jnp.logical_and(mask, cj < tq)

            _softmax_block(qqs[h], _pad_keys(kn_ref[:, hs], tkn), _pad_keys(vn_ref[:, hs], tkn),
                           lambda r0, j: diag(r0, j)[0], lambda r0, j: diag(r0, j)[1], _head_state(st, h))
            o_ref[:, hs] = _diff_finish(st[4][h], lamp_ref, gain_ref, tq, lam_init).astype(o_ref.dtype)


def _fox_sample_kernel(q_ref, kn_ref, vn_ref, kp_ref, vp_ref, nckp_ref, nckn_ref, o_ref, *st, tq, tk, tkn, nh, dh):
    kb_i = pl.program_id(1)

    @pl.when(kb_i == 0)
    def _():
        _state_init(st)

    q = q_ref[...]
    for h in range(nh):
        bias = nckp_ref[h]
        _softmax_block(q[:, h * dh:(h + 1) * dh], _head_rows(kp_ref, h, tk, nh), _head_rows(vp_ref, h, tk, nh),
                       lambda r0, j: _lane_tile(bias, j), None, _head_state(st, h))

    @pl.when(kb_i == pl.num_programs(1) - 1)
    def _():
        col = lax.broadcasted_iota(jnp.int32, (1, tkn), 1)
        for h in range(nh):
            hs = slice(h * dh, (h + 1) * dh)
            bias = nckn_ref[h]
            _softmax_block(q[:, hs], _pad_keys(kn_ref[:, hs], tkn), _pad_keys(vn_ref[:, hs], tkn),
                           lambda r0, j: _lane_tile(bias, j),
                           lambda r0, j: _lane_tile(col, j) <= _chunk_rows(r0), _head_state(st, h))
            o_ref[:, hs] = _attn_out(st[4][h], dh).astype(o_ref.dtype)


def _sample_attention(kind, q, kv, cache_k, cache_v, layer, row0, b, t, nh, dh, q_col0, k_seg, v_seg, extra,
                      lam_init=None):
    past = cache_k.shape[2]
    cache_k, cache_v = (c.reshape(c.shape[0], c.shape[1], past * nh, dh) for c in (cache_k, cache_v))
    tk = _pick(past, (1024, 512, 256, 128))
    nkb = past // tk
    rb = row0 // t
    w = nh * dh
    qcb = q_col0 * dh // w
    q_spec = pl.BlockSpec((t, w), lambda bi, kb: (rb + bi, qcb))
    kn_spec = pl.BlockSpec((None, t, w), lambda bi, kb: (k_seg, rb + bi, 0))
    vn_spec = pl.BlockSpec((None, t, w), lambda bi, kb: (v_seg, rb + bi, 0))
    kp_spec = pl.BlockSpec((None, None, tk * nh, dh), lambda bi, kb: (layer, bi, kb, 0))
    o_spec = pl.BlockSpec((t, w), lambda bi, kb: (bi, 0))
    rows = 2 * t if kind == "diff" else t
    tkn = -(-t // LANE) * LANE
    scratch = _attn_scratch(rows, max(tk, tkn), dh, nh)
    vm = 2 * (2 * tk * nh * 2 * dh * 4 + 4 * t * w * 2) + 4 * nh * rows * LANE * 4 + 16 * rows * tk * 4 + 8 * MIB
    if kind == "diff":
        slopes, lamp, gain = extra
        kern = functools.partial(_diff_sample_kernel, tq=t, tk=tk, tkn=tkn, past=past, nh=nh, dh=dh,
                                 lam_init=lam_init)
        in_specs = [pl.BlockSpec(memory_space=pltpu.SMEM),
                    pl.BlockSpec(lamp.shape, lambda bi, kb: (0, 0)),
                    pl.BlockSpec(gain.shape, lambda bi, kb: (0, 0)),
                    q_spec, kn_spec, vn_spec, kp_spec, kp_spec]
        args = (slopes, lamp, gain, q, kv, kv, cache_k, cache_v)
    else:
        nckp, nckn = extra
        nckn = jnp.pad(nckn, ((0, 0), (0, 0), (0, 0), (0, tkn - t)))
        kern = functools.partial(_fox_sample_kernel, tq=t, tk=tk, tkn=tkn, nh=nh, dh=dh)
        in_specs = [q_spec, kn_spec, vn_spec, kp_spec, kp_spec,
                    pl.BlockSpec((None, nh, 1, tk), lambda bi, kb: (bi, 0, 0, kb)),
                    pl.BlockSpec((None, nh, 1, tkn), lambda bi, kb: (bi, 0, 0, 0))]
        args = (q, kv, kv, cache_k, cache_v, nckp, nckn)
    return pl.pallas_call(
        kern,
        grid=(b, nkb),
        in_specs=in_specs,
        out_specs=o_spec,
        out_shape=jax.ShapeDtypeStruct((b * t, w), BF),
        scratch_shapes=scratch,
        compiler_params=_cp(("parallel", "arbitrary"), vm),
        name=kind + "_sample_attn",
    )(*args)


def _expm1(x):
    u = jnp.exp(x)
    um1 = u - 1.0
    return jnp.where(um1 == 0.0, x, jnp.where(um1 == -1.0, -1.0, um1 * x / jnp.log(u)))


def _rglru_kernel(rx_ref, rg_ref, hist_ref, h0_ref, cw_ref, cb_ref, wa_ref, wi_ref, ba_ref, bi_ref, lam_ref,
                  yc_ref, hl_ref, cn_ref, prev_sc, h_sc, *, tt, cwidth):
    j = pl.program_id(2)

    @pl.when(j == 0)
    def _():
        prev_sc[...] = hist_ref[...]
        h_sc[...] = h0_ref[...]

    x = rx_ref[...]
    xp = jnp.concatenate([prev_sc[...], x], axis=0)
    cw = cw_ref[...]
    xc = cb_ref[...]
    for jj in range(cwidth):
        s0 = 8 - (cwidth - 1) + jj
        xc = xc + xp[s0:s0 + tt] * cw[jj:jj + 1]
    xcb = xc.astype(BF)
    r = jax.nn.sigmoid(jnp.dot(xcb, wa_ref[...], preferred_element_type=F32) + ba_ref[...])
    ig = jax.nn.sigmoid(jnp.dot(xcb, wi_ref[...], preferred_element_type=F32) + bi_ref[...])
    log_a = -LRU_C * r * jax.nn.softplus(-lam_ref[...])
    a = jnp.exp(log_a)
    bb = jnp.sqrt(-_expm1(2.0 * log_a)) * (ig * xc)

    row = lax.broadcasted_iota(jnp.int32, (tt, 1), 0)
    d = 1
    while d < tt:
        keep = row >= d
        a_sh = jnp.where(keep, pltpu.roll(a, d, 0), 1.0)
        b_sh = jnp.where(keep, pltpu.roll(bb, d, 0), 0.0)
        bb = a * b_sh + bb
        a = a * a_sh
        d *= 2
    hseq = bb + a * h_sc[...]

    yc_ref[...] = (hseq * rg_ref[...].astype(F32)).astype(yc_ref.dtype)
    h_sc[...] = hseq[tt - 1:tt]
    prev_sc[...] = xp[tt:tt + 8]

    @pl.when(j == pl.num_programs(2) - 1)
    def _():
        hl_ref[...] = hseq[tt - 1:tt]
        cn_ref[...] = xp[tt:tt + 8]


def _rglru(rx, rgg, hist8, h0, prm, layer, row0, b, t):
    c = rx.shape[1]
    cw, cb, wa, wi, ba, bi, lam = prm
    ct = wa.shape[2]
    nc = c // ct
    tt = _pick(t, (256, 128, 64, 32, 16, 8))
    nt = t // tt
    rb = row0 // tt
    cwidth = cw.shape[0]
    vec = lambda: pl.BlockSpec((1, ct), lambda bi_, ci, j: (0, ci))
    row_in = pl.BlockSpec((tt, ct), lambda bi_, ci, j: (rb + bi_ * nt + j, ci))
    wsp = pl.BlockSpec((None, None, ct, ct), lambda bi_, ci, j: (layer, ci, 0, 0))
    return pl.pallas_call(
        functools.partial(_rglru_kernel, tt=tt, cwidth=cwidth),
        grid=(b, nc, nt),
        in_specs=[row_in, row_in,
                  pl.BlockSpec((None, 8, ct), lambda bi_, ci, j: (bi_, 0, ci)),
                  pl.BlockSpec((None, 1, ct), lambda bi_, ci, j: (bi_, 0, ci)),
                  pl.BlockSpec((cwidth, ct), lambda bi_, ci, j: (0, ci)),
                  vec(), wsp, wsp, vec(), vec(), vec()],
        out_specs=[pl.BlockSpec((tt, ct), lambda bi_, ci, j: (bi_ * nt + j, ci)),
                   pl.BlockSpec((None, 1, ct), lambda bi_, ci, j: (bi_, 0, ci)),
                   pl.BlockSpec((None, 8, ct), lambda bi_, ci, j: (bi_, 0, ci))],
        out_shape=[jax.ShapeDtypeStruct((b * t, c), BF),
                   jax.ShapeDtypeStruct((b, 1, c), F32),
                   jax.ShapeDtypeStruct((b, 8, c), F32)],
        scratch_shapes=[pltpu.VMEM((8, ct), F32), pltpu.VMEM((1, ct), F32)],
        compiler_params=_cp(("parallel", "parallel", "arbitrary"), 32 * tt * ct * 4 + 8 * MIB),
        name="rglru",
    )(rx, rgg, hist8, h0, cw, cb.reshape(1, c), wa, wi, ba.reshape(1, c), bi.reshape(1, c), lam.reshape(1, c))


def _merge_kernel(ya_ref, yb_ref, yc_ref, g0_ref, g1_ref, g2_ref, wa_ref, wb_ref, wc_ref, o_ref):
    pa = jnp.dot(ya_ref[...], wa_ref[...], preferred_element_type=F32)
    pb = jnp.dot(yb_ref[...], wb_ref[...], preferred_element_type=F32)
    pc = jnp.dot(yc_ref[...], wc_ref[...], preferred_element_type=F32)
    o_ref[...] = (g0_ref[...].astype(F32) * pa + g1_ref[...].astype(F32) * pb
                  + g2_ref[...].astype(F32) * pc).astype(o_ref.dtype)


def _merge(ya, yb, yc, gates, wpa, wpb, wpc, layer):
    n = ya.shape[0]
    d = wpa.shape[2]
    tm = _pick(n, (512, 256, 128, 64, 8))
    tn = _pick(d, (512, 256, 128))
    nd = d // tn
    act = lambda a: pl.BlockSpec((tm, a.shape[1]), lambda i, j: (i, 0))
    wsp = lambda w: pl.BlockSpec((None, w.shape[1], tn), lambda i, j: (layer, 0, j))
    gsp = lambda s: pl.BlockSpec((tm, tn), lambda i, j: (i, s * nd + j))
    return pl.pallas_call(
        _merge_kernel,
        grid=(n // tm, nd),
        in_specs=[act(ya), act(yb), act(yc), gsp(0), gsp(1), gsp(2), wsp(wpa), wsp(wpb), wsp(wpc)],
        out_specs=pl.BlockSpec((tm, tn), lambda i, j: (i, j)),
        out_shape=jax.ShapeDtypeStruct((n, d), BF),
        compiler_params=_cp(("parallel", "parallel"), 32 * MIB),
        name="merge",
    )(ya, yb, yc, gates, gates, gates, wpa, wpb, wpc)


def _mm_res_norm_kernel(x_ref, w_ref, h_ref, g_ref, ho_ref, xn_ref):
    hn = h_ref[...] + jnp.dot(x_ref[...], w_ref[...], preferred_element_type=F32)
    ho_ref[...] = hn
    xn_ref[...] = _rms(hn, g_ref[...]).astype(xn_ref.dtype)


def _mm_res_norm(x, w, layer, h, g, xn_dtype):
    n, k = x.shape
    d = w.shape[2]
    tm = _pick(n, (256, 128, 64, 8))
    row = lambda c: pl.BlockSpec((tm, c), lambda i: (i, 0))
    return pl.pallas_call(
        _mm_res_norm_kernel,
        grid=(n // tm,),
        in_specs=[row(k), pl.BlockSpec((None, k, d), lambda i: (layer, 0, 0)), row(d),
                  pl.BlockSpec((1, d), lambda i: (0, 0))],
        out_specs=[row(d), row(d)],
        out_shape=[jax.ShapeDtypeStruct((n, d), F32), jax.ShapeDtypeStruct((n, d), xn_dtype)],
        compiler_params=_cp(("parallel",), 2 * (k * d * 2 + tm * k * 2 + 3 * tm * d * 4) + 8 * MIB),
        name="out_proj",
    )(x, w, h, g.reshape(1, d))


def _ffn_kernel(x_ref, wg_ref, wu_ref, wd_ref, h_ref, g_ref, ho_ref, xn_ref, acc_sc):
    f = pl.program_id(1)

    @pl.when(f == 0)
    def _():
        acc_sc[...] = jnp.zeros_like(acc_sc)

    x = x_ref[...]
    a = (jax.nn.silu(jnp.dot(x, wg_ref[...], preferred_element_type=F32))
         * jnp.dot(x, wu_ref[...], preferred_element_type=F32)).astype(BF)
    acc_sc[...] += jnp.dot(a, wd_ref[...], preferred_element_type=F32)

    @pl.when(f == pl.num_programs(1) - 1)
    def _():
        hn = h_ref[...] + acc_sc[...]
        ho_ref[...] = hn
        xn_ref[...] = _rms(hn, g_ref[...]).astype(xn_ref.dtype)


def _ffn(x, wg, wu, wd, layer, h, g):
    n, d = x.shape
    ff = wg.shape[2]
    tm = _pick(n, (512, 256, 128, 64, 8))
    tf = _pick(ff, (512, 256, 128))
    row = pl.BlockSpec((tm, d), lambda i, f: (i, 0))
    vm = 2 * (3 * d * tf * 2 + tm * d * (2 + 4 + 4 + 2)) + tm * d * 4 + 3 * tm * tf * 4 + 4 * MIB
    return pl.pallas_call(
        _ffn_kernel,
        grid=(n // tm, ff // tf),
        in_specs=[row,
                  pl.BlockSpec((None, d, tf), lambda i, f: (layer, 0, f)),
                  pl.BlockSpec((None, d, tf), lambda i, f: (layer, 0, f)),
                  pl.BlockSpec((None, tf, d), lambda i, f: (layer, f, 0)),
                  row, pl.BlockSpec((1, d), lambda i, f: (0, 0))],
        out_specs=[row, row],
        out_shape=[jax.ShapeDtypeStruct((n, d), F32), jax.ShapeDtypeStruct((n, d), BF)],
        scratch_shapes=[pltpu.VMEM((tm, d), F32)],
        compiler_params=_cp(("parallel", "arbitrary"), vm),
        name="ffn",
    )(x, wg, wu, wd, h, g.reshape(1, d))


def _router_kernel(x_ref, wh_ref, wl_ref, b_ref, e_ref, p_ref, *, n_exp):
    x = x_ref[...]
    xh = x.astype(BF)
    xl = (x - xh.astype(F32)).astype(BF)
    wh = wh_ref[...]
    logits = (jnp.dot(xh, wh, preferred_element_type=F32) + jnp.dot(xl, wh, preferred_element_type=F32)
              + jnp.dot(xh, wl_ref[...], preferred_element_type=F32) + b_ref[...])
    lane = lax.broadcasted_iota(jnp.int32, logits.shape, 1)
    logits = jnp.where(lane < n_exp, logits, NEG)
    m1 = jnp.max(logits, axis=-1, keepdims=True)
    i1 = jnp.min(jnp.where(logits == m1, lane, LANE), axis=-1, keepdims=True)
    rest = jnp.where(lane == i1, NEG, logits)
    m2 = jnp.max(rest, axis=-1, keepdims=True)
    i2 = jnp.min(jnp.where(rest == m2, lane, LANE), axis=-1, keepdims=True)
    e2 = jnp.exp(m2 - m1)
    p1 = 1.0 / (1.0 + e2)
    p2 = e2 / (1.0 + e2)
    e_ref[...] = jnp.where(lane == 0, i1, jnp.where(lane == 1, i2, 0))
    p_ref[...] = jnp.where(lane == 0, p1, jnp.where(lane == 1, p2, 0.0))


def _router(x, wr, br):
    n, d = x.shape
    n_exp = wr.shape[1]
    tm = _pick(n, (256, 128, 64, 8))
    wpad = jnp.zeros((d, LANE), F32).at[:, :n_exp].set(wr)
    wh = wpad.astype(BF)
    wl = (wpad - wh.astype(F32)).astype(BF)
    bpad = jnp.zeros((1, LANE), F32).at[0, :n_exp].set(br)
    row = lambda: pl.BlockSpec((tm, LANE), lambda i: (i, 0))
    return pl.pallas_call(
        functools.partial(_router_kernel, n_exp=n_exp),
        grid=(n // tm,),
        in_specs=[pl.BlockSpec((tm, d), lambda i: (i, 0)),
                  pl.BlockSpec((d, LANE), lambda i: (0, 0)),
                  pl.BlockSpec((d, LANE), lambda i: (0, 0)),
                  pl.BlockSpec((1, LANE), lambda i: (0, 0))],
        out_specs=[row(), row()],
        out_shape=[jax.ShapeDtypeStruct((n, LANE), jnp.int32), jax.ShapeDtypeStruct((n, LANE), F32)],
        compiler_params=_cp(("parallel",), 16 * MIB),
        name="moe_router",
    )(x, wh, wl, bpad)


def _moe_ffn_kernel(te_ref, nu_ref, rt_ref, x_hbm, wg_ref, wu_ref, wd_ref, y_ref, xf_sc, xb_sc, acc_sc, sem,
                    *, tm):
    i = pl.program_id(0)
    f = pl.program_id(1)

    def row_copy(r, tok):
        return pltpu.make_async_copy(x_hbm.at[pl.ds(tok, 1)], xf_sc.at[pl.ds(r, 1)], sem.at[0])

    @pl.when(i < nu_ref[0])
    def _():
        @pl.when(f == 0)
        def _():
            def issue(r, c):
                row_copy(r, rt_ref[0, r]).start()
                return c

            def drain(r, c):
                row_copy(r, 0).wait()
                return c

            lax.fori_loop(0, tm, issue, 0)
            lax.fori_loop(0, tm, drain, 0)
            xb_sc[...] = xf_sc[...].astype(BF)
            acc_sc[...] = jnp.zeros_like(acc_sc)

        x = xb_sc[...]
        a = (jax.nn.silu(jnp.dot(x, wg_ref[...], preferred_element_type=F32))
             * jnp.dot(x, wu_ref[...], preferred_element_type=F32)).astype(BF)
        acc_sc[...] += jnp.dot(a, wd_ref[...], preferred_element_type=F32)

        @pl.when(f == pl.num_programs(1) - 1)
        def _():
            y_ref[...] = acc_sc[...]

    @pl.when(jnp.logical_and(i >= nu_ref[0], f == pl.num_programs(1) - 1))
    def _():
        y_ref[...] = jnp.zeros_like(y_ref)


def _moe_ffn(x, tile_expert, row_token, n_used, wg, wu, wd, layer, tm):
    d = x.shape[1]
    ff = wg.shape[3]
    nt = tile_expert.shape[0]
    tf = _pick(ff, (256, 128))
    nf = ff // tf

    def tile(i, nu):
        return jnp.minimum(i, nu[0] - 1)

    def fidx(i, f, nu):
        return jnp.where(i < nu[0], f, nf - 1)

    grid_spec = pltpu.PrefetchScalarGridSpec(
        num_scalar_prefetch=2,
        grid=(nt, nf),
        in_specs=[pl.BlockSpec((None, 1, tm), lambda i, f, te, nu: (i, 0, 0), memory_space=pltpu.SMEM),
                  pl.BlockSpec(memory_space=pl.ANY),
                  pl.BlockSpec((None, None, d, tf),
                               lambda i, f, te, nu: (layer, te[tile(i, nu)], 0, fidx(i, f, nu))),
                  pl.BlockSpec((None, None, d, tf),
                               lambda i, f, te, nu: (layer, te[tile(i, nu)], 0, fidx(i, f, nu))),
                  pl.BlockSpec((None, None, tf, d),
                               lambda i, f, te, nu: (layer, te[tile(i, nu)], fidx(i, f, nu), 0))],
        out_specs=pl.BlockSpec((tm, d), lambda i, f, te, nu: (i, 0)),
        scratch_shapes=[pltpu.VMEM((tm, d), F32), pltpu.VMEM((tm, d), BF), pltpu.VMEM((tm, d), F32),
                        pltpu.SemaphoreType.DMA((1,))],
    )
    vm = 2 * (3 * d * tf * 2 + tm * d * 4) + tm * d * 10 + 3 * tm * tf * 4 + 4 * MIB
    return pl.pallas_call(
        functools.partial(_moe_ffn_kernel, tm=tm),
        grid_spec=grid_spec,
        out_shape=jax.ShapeDtypeStruct((nt * tm, d), F32),
        compiler_params=_cp(("arbitrary", "arbitrary"), vm),
        name="moe_ffn",
    )(tile_expert, n_used, row_token.reshape(nt, 1, tm), x, wg, wu, wd)


def _moe_combine_kernel(pos_ref, y_hbm, p_ref, h_ref, g_ref, ho_ref, xn_ref, y_sc, sem, *, tm):
    def row_copy(k, r, src):
        return pltpu.make_async_copy(y_hbm.at[pl.ds(src, 1)], y_sc.at[k, pl.ds(r, 1)], sem.at[0])

    def issue(r, c):
        row_copy(0, r, pos_ref[0, 2 * r]).start()
        row_copy(1, r, pos_ref[0, 2 * r + 1]).start()
        return c

    def drain(r, c):
        row_copy(0, r, 0).wait()
        row_copy(1, r, 0).wait()
        return c

    lax.fori_loop(0, tm, issue, 0)
    lax.fori_loop(0, tm, drain, 0)
    p = p_ref[...]
    hn = h_ref[...] + (p[:, 0:1] * y_sc[0] + p[:, 1:2] * y_sc[1])
    ho_ref[...] = hn
    xn_ref[...] = _rms(hn, g_ref[...]).astype(xn_ref.dtype)


def _moe_combine(pos, y, p, h, g):
    n, d = h.shape
    tm = _pick(n, (256, 128, 64, 8))
    row = lambda c: pl.BlockSpec((tm, c), lambda i: (i, 0))
    return pl.pallas_call(
        functools.partial(_moe_combine_kernel, tm=tm),
        grid=(n // tm,),
        in_specs=[pl.BlockSpec((None, 1, 2 * tm), lambda i: (i, 0, 0), memory_space=pltpu.SMEM),
                  pl.BlockSpec(memory_space=pl.ANY), row(LANE), row(d),
                  pl.BlockSpec((1, d), lambda i: (0, 0))],
        out_specs=[row(d), row(d)],
        scratch_shapes=[pltpu.VMEM((2, tm, d), F32), pltpu.SemaphoreType.DMA((1,))],
        out_shape=[jax.ShapeDtypeStruct((n, d), F32), jax.ShapeDtypeStruct((n, d), BF)],
        compiler_params=_cp(("arbitrary",), 2 * (tm * d * 10 + tm * LANE * 4) + 2 * tm * d * 4 + 8 * MIB),
        name="moe_combine",
    )(pos.reshape(n // tm, 1, 2 * tm), y, p, h, g.reshape(1, d))


def _moe(xn, h, wr, br, wg, wu, wd, layer, g_next):
    n = xn.shape[0]
    n_exp = wr.shape[1]
    tm = _pick(n, (512, 256, 128, 64, 8))
    e_pad, p_pad = _router(xn, wr, br)
    top_e = e_pad[:, :2]
    onehot = (top_e[:, :, None] == jnp.arange(n_exp, dtype=jnp.int32)).astype(jnp.int32).sum(axis=1)
    counts = onehot.sum(axis=0)
    padded = ((counts + tm - 1) // tm) * tm
    ends = jnp.cumsum(padded)
    starts = ends - padded
    rank = jnp.cumsum(onehot, axis=0) - onehot
    pos = starts[top_e] + jnp.take_along_axis(rank, top_e, axis=1)
    nt = (2 * n + n_exp * (tm - 1) + tm - 1) // tm
    row_token = jnp.zeros((nt * tm,), jnp.int32).at[pos.reshape(-1)].set(
        jnp.repeat(jnp.arange(n, dtype=jnp.int32), 2))
    tile_start = jnp.arange(nt, dtype=jnp.int32) * tm
    tile_expert = jnp.minimum((tile_start[:, None] >= ends[None, :]).astype(jnp.int32).sum(axis=1), n_exp - 1)
    n_used = (ends[-1:] // tm).astype(jnp.int32)
    y = _moe_ffn(xn, tile_expert, row_token, n_used, wg, wu, wd, layer, tm)
    return _moe_combine(pos.reshape(-1).astype(jnp.int32), y, p_pad, h, g_next)


def _ple_kernel(x_ref, p_ref, wg_ref, wu_ref, h_ref, g_ref, *o_refs):
    gate = jax.nn.sigmoid(jnp.dot(x_ref[...], wg_ref[...], preferred_element_type=F32))
    up = jnp.dot(p_ref[...].astype(BF), wu_ref[...], preferred_element_type=F32)
    hn = h_ref[...] + gate * up
    xn = _rms(hn, g_ref[...])
    if len(o_refs) == 2:
        o_refs[0][...] = hn
    o_refs[-1][...] = xn.astype(o_refs[-1].dtype)


def _ple(x, p, wg, wu, layer, h, g, xn_dtype, want_h):
    n, d = x.shape
    dp = p.shape[1]
    tm = _pick(n, (256, 128, 64, 8))
    row = lambda c: pl.BlockSpec((tm, c), lambda i: (i, 0))
    wsp = lambda a: pl.BlockSpec((None,) + a.shape[1:], lambda i: (layer, 0, 0))
    out_specs = ([row(d)] if want_h else []) + [row(d)]
    out_shape = ([jax.ShapeDtypeStruct((n, d), F32)] if want_h else []) + [jax.ShapeDtypeStruct((n, d), xn_dtype)]
    outs = pl.pallas_call(
        _ple_kernel,
        grid=(n // tm,),
        in_specs=[row(d), row(dp), wsp(wg), wsp(wu), row(d), pl.BlockSpec((1, d), lambda i: (0, 0))],
        out_specs=out_specs,
        out_shape=out_shape,
        compiler_params=_cp(("parallel",), 2 * (d * d * 2 + dp * d * 2 + tm * d * 14 + tm * dp * 4) + 8 * MIB),
        name="ple",
    )(x, p, wg, wu, h, g.reshape(1, d))
    return (outs[0], outs[1]) if want_h else (None, outs[0])


def _block_diag(w, group):
    nl, nb, r, _ = w.shape
    eye = jnp.eye(group, dtype=w.dtype)
    return jnp.einsum("ab,lcaij->lcaibj", eye, w.reshape(nl, nb // group, group, r, r)).reshape(
        nl, nb // group, group * r, group * r)


def kernel(x_prompt, x_sample, cache_diff_k, cache_diff_v, cache_fox_k, cache_fox_v, cache_fox_logf, state_rnn_h, state_rnn_conv, p_prompt, p_sample, norm_mix, w_in, b_in, lam_q1, lam_k1, lam_q2, lam_k2, diff_gain, conv_w, conv_b, w_a, b_a, w_i, b_i, lru_lambda, w_pa, w_pb, w_pc, w_o, norm_ffn, ffn_wg, ffn_wu, ffn_wd, moe_wr, moe_br, moe_wg, moe_wu, moe_wd, ple_norm, ple_up, ple_gate, final_norm):
    bp, tp, d = x_prompt.shape
    bs, ts, _ = x_sample.shape
    depth = w_in.shape[0]
    past = cache_diff_k.shape[2]
    nh, dh = cache_diff_k.shape[3], cache_diff_k.shape[4]
    seg = nh * dh
    c_rnn = state_rnn_h.shape[2]
    cwidth = conv_w.shape[1]
    assert cache_diff_v.shape[3:] == cache_fox_k.shape[3:] == cache_fox_v.shape[3:] == (nh, dh) and dh == LANE
    assert past % (1 << CHUNK_SHIFT) == 0 and ts <= (1 << CHUNK_SHIFT) and cwidth - 1 <= 8
    np_, ns_ = bp * tp, bs * ts
    n = np_ + ns_
    assert np_ % ts == 0

    o_ff = 6 * seg
    o_rx = o_ff + nh
    assert w_in.shape[2] == o_rx + 2 * c_rnn + 3 * d

    def cols(a, lo, width):
        return lax.slice_in_dim(a, lo, lo + width, axis=a.ndim - 1)

    w_qkv = cols(w_in, 0, o_ff).astype(BF)
    w_rest = cols(w_in, o_rx, 2 * c_rnn + 3 * d).astype(BF)
    w_ff = jnp.pad(cols(w_in, o_ff, nh), ((0, 0), (0, 0), (0, LANE - nh))).astype(BF)
    w_pa16, w_pb16, w_pc16, w_o16 = (a.astype(BF) for a in (w_pa, w_pb, w_pc, w_o))
    ffn16 = tuple(a.astype(BF) for a in (ffn_wg, ffn_wu, ffn_wd))
    moe16 = tuple(a.astype(BF) for a in (moe_wg, moe_wu, moe_wd))
    ple_gate16, ple_up16 = ple_gate.astype(BF), ple_up.astype(BF)
    grp = (RNN_CT if c_rnn % RNN_CT == 0 else c_rnn) // w_a.shape[2]
    wa16, wi16 = _block_diag(w_a, grp).astype(BF), _block_diag(w_i, grp).astype(BF)

    slopes2 = jnp.asarray([LOG2E * 2.0 ** (-8.0 * (k + 1) / nh) for k in range(nh)], F32)
    q_scale = LOG2E * jnp.concatenate([jnp.full((seg,), (dh // 2) ** -0.5, F32), jnp.full((seg,), dh ** -0.5, F32)])

    w0 = 8 - (cwidth - 1)
    hist8_s = jnp.pad(state_rnn_conv, ((0, 0), (0, 0), (w0, 0), (0, 0)))
    hist8_p = jnp.zeros((bp, 8, c_rnn), F32)
    h0_p = jnp.zeros((bp, 1, c_rnn), F32)
    kv_p = tuple(jnp.zeros((depth, np_, nh, dh), F32) for _ in range(4))
    kv_s = tuple(jnp.zeros((depth, ns_, nh, dh), F32) for _ in range(4))

    h = jnp.concatenate([x_prompt.reshape(np_, d), x_sample.reshape(ns_, d)], axis=0)
    xn = _rmsnorm(h, norm_mix[0], BF)
    st_p, st_s = [], []
    y = None
    for i in range(depth):
        bi_l = b_in[i]
        b_q = jnp.concatenate([cols(bi_l, 0, seg), cols(bi_l, 3 * seg, seg)])
        q16 = _mm(xn, w_qkv, i, 0, 3 * (seg // _pick(2 * seg, (512, 256, 128))), b_q, BF, act="scale",
                  scale=q_scale)
        b_kv = jnp.concatenate([cols(bi_l, seg, 2 * seg), cols(bi_l, 4 * seg, 2 * seg)])
        kv16, kv_p, kv_s = _kv_proj(xn, w_qkv, i, b_kv, kv_p, kv_s, np_, nh, dh)
        b_ff = jnp.pad(cols(bi_l, o_ff, nh), (0, LANE - nh))
        logf = _mm(xn, w_ff, i, 0, 1, b_ff, F32, act="log_sigmoid")[:, :nh]
        rx = _mm(xn, w_rest, i, 0, 1, cols(bi_l, o_rx, c_rnn), F32)
        rgg = _mm(xn, w_rest, i, c_rnn // _pick(c_rnn, (512, 256, 128)), 1, cols(bi_l, o_rx + c_rnn, c_rnn), BF,
                  act="gelu")
        gates = _mm(xn, w_rest, i, 2 * c_rnn // _pick(3 * d, (512, 256, 128)), 1,
                    cols(bi_l, o_rx + 2 * c_rnn, 3 * d), BF, act="sigmoid")

        lam_init = 0.8 - 0.6 * math.exp(-0.3 * i)
        lamp = jnp.stack([lam_q1[i], lam_k1[i], lam_q2[i], lam_k2[i]])
        dextra = (slopes2, lamp, diff_gain[i].reshape(1, -1))
        ya_p = _prompt_attention("diff", q16, kv16, bp, tp, nh, dh, 0, 0, 1, dextra, lam_init)
        ya_s = _sample_attention("diff", q16, kv16, cache_diff_k, cache_diff_v, i, np_, bs, ts, nh, dh, 0, 0, 1,
                                 dextra, lam_init)
        logf_p = logf[:np_].reshape(bp, tp, nh)
        logf_s = logf[np_:].reshape(bs, ts, nh)
        cum_p = jnp.cumsum(logf_p, axis=1)
        cum_past = jnp.cumsum(cache_fox_logf[i].astype(F32), axis=1)
        cum_s = cum_past[:, -1:] + jnp.cumsum(logf_s, axis=1)
        as_rows = lambda c: (-LOG2E) * jnp.transpose(c, (0, 2, 1))[:, :, None, :]
        yb_p = _prompt_attention("fox", q16, kv16, bp, tp, nh, dh, nh, 2, 3, (as_rows(cum_p),))
        yb_s = _sample_attention("fox", q16, kv16, cache_fox_k, cache_fox_v, i, np_, bs, ts, nh, dh, nh, 2, 3,
                                 (as_rows(cum_past), as_rows(cum_s)))
        rprm = (conv_w[i], conv_b[i], wa16, wi16, b_a[i], b_i[i], lru_lambda[i])
        yc_p, hl_p, cn_p = _rglru(rx, rgg, hist8_p, h0_p, rprm, i, 0, bp, tp)
        yc_s, hl_s, cn_s = _rglru(rx, rgg, hist8_s[i], state_rnn_h[i][:, None, :], rprm, i, np_, bs, ts)
        cat = lambda a, b_: jnp.concatenate([a, b_], axis=0)
        merged = _merge(cat(ya_p, ya_s), cat(yb_p, yb_s), cat(yc_p, yc_s), gates, w_pa16, w_pb16, w_pc16, i)
        j = i // 2
        is_moe = i % 2 == 1
        h, xn2 = _mm_res_norm(merged, w_o16, i, h, norm_ffn[i], F32 if is_moe else BF)
        if is_moe:
            h, xn3 = _moe(xn2, h, moe_wr[j], moe_br[j], *moe16, j, ple_norm[i])
        else:
            h, xn3 = _ffn(xn2, *ffn16, j, h, ple_norm[i])
        p_cat = jnp.concatenate([p_prompt[i].reshape(np_, -1), p_sample[i].reshape(ns_, -1)], axis=0)
        last = i == depth - 1
        g_next = final_norm if last else norm_mix[i + 1]
        h, xn = _ple(xn3, p_cat, ple_gate16, ple_up16, i, h, g_next, F32 if last else BF, not last)
        if last:
            y = xn
        st_p.append((logf_p, hl_p[:, 0], cn_p[:, w0:]))
        st_s.append((logf_s, hl_s[:, 0], cn_s[:, w0:]))

    def small(sts):
        return tuple(jnp.stack([s[k] for s in sts]) for k in range(3))

    def rows5(a, b, t):
        return a.reshape(depth, b, t, nh, dh)

    return ((y[:np_].reshape(bp, tp, d), y[np_:].reshape(bs, ts, d))
            + tuple(rows5(a, bp, tp) for a in kv_p) + small(st_p)
            + tuple(rows5(a, bs, ts) for a in kv_s) + small(st_s))
```

```python
import functools
import math

import jax
import jax.numpy as jnp
from jax import lax
from jax.experimental import pallas as pl
from jax.experimental.pallas import tpu as pltpu

F32 = jnp.float32
BF = jnp.bfloat16

EPS = 1e-6
CHUNK_SHIFT = 6
LRU_C = 8.0
NEG = -1e30
LOG2E = 1.4426950408889634
MIB = 1024 * 1024
V7X_VMEM_CAP = 60 * MIB
LANE = 128
RNN_CT = 256
ATT_RG = 256
ATT_RC = 32


def _pick(n, cands):
    for c in cands:
        if n % c == 0:
            return c
    raise ValueError(f"no tile in {cands} divides {n}")


def _cp(sem, vmem_bytes):
    return pltpu.CompilerParams(dimension_semantics=sem,
                                vmem_limit_bytes=int(min(max(vmem_bytes, 16 * MIB), V7X_VMEM_CAP)))


def _rms(xf, g):
    return xf * lax.rsqrt(jnp.mean(xf * xf, axis=-1, keepdims=True) + EPS) * g


def _rmsnorm_kernel(x_ref, g_ref, o_ref):
    o_ref[...] = _rms(x_ref[...], g_ref[...]).astype(o_ref.dtype)


def _rmsnorm(x, g, out_dtype):
    n, d = x.shape
    tm = _pick(n, (512, 256, 128, 64, 8))
    return pl.pallas_call(
        _rmsnorm_kernel,
        grid=(n // tm,),
        in_specs=[pl.BlockSpec((tm, d), lambda i: (i, 0)), pl.BlockSpec((1, d), lambda i: (0, 0))],
        out_specs=pl.BlockSpec((tm, d), lambda i: (i, 0)),
        out_shape=jax.ShapeDtypeStruct((n, d), out_dtype),
        compiler_params=_cp(("parallel",), 6 * tm * d * 4),
        name="rmsnorm",
    )(x, g.reshape(1, d))


def _mm_kernel(x_ref, w_ref, b_ref, s_ref, o_ref, *, act):
    acc = jnp.dot(x_ref[...], w_ref[...], preferred_element_type=F32) + b_ref[...]
    if act == "sigmoid":
        acc = jax.nn.sigmoid(acc)
    elif act == "gelu":
        acc = jax.nn.gelu(acc)
    elif act == "log_sigmoid":
        acc = jax.nn.log_sigmoid(acc)
    elif act == "scale":
        acc = acc * s_ref[...]
    o_ref[...] = acc.astype(o_ref.dtype)


def _mm(x, w, layer, col0, stride, b, out_dtype, act=None, scale=None):
    m, k = x.shape
    n = b.shape[0]
    tm = _pick(m, (1024, 512, 256, 128, 64, 8))
    tn = _pick(n, (512, 256, 128))
    if scale is None:
        scale = jnp.ones((n,), F32)
    return pl.pallas_call(
        functools.partial(_mm_kernel, act=act),
        grid=(m // tm, n // tn),
        in_specs=[pl.BlockSpec((tm, k), lambda i, j: (i, 0)),
                  pl.BlockSpec((None, k, tn), lambda i, j: (layer, 0, col0 + stride * j)),
                  pl.BlockSpec((1, tn), lambda i, j: (0, j)),
                  pl.BlockSpec((1, tn), lambda i, j: (0, j))],
        out_specs=pl.BlockSpec((tm, tn), lambda i, j: (i, j)),
        out_shape=jax.ShapeDtypeStruct((m, n), out_dtype),
        compiler_params=_cp(("parallel", "parallel"),
                            2 * (tm * k * 2 + k * tn * 2 + tm * tn * 4) + 3 * tm * tn * 4),
        name="mm_" + (act or "lin"),
    )(x, w, b.reshape(1, n).astype(F32), scale.reshape(1, n).astype(F32))


def _kv_kernel(x_ref, w_ref, b_ref, *refs, nseg, npt, nh, dh):
    outs = refs[2 * nseg:]
    kv16_ref, state_refs = outs[0], outs[1:]
    i = pl.program_id(0)
    j = pl.program_id(1)
    acc = jnp.dot(x_ref[...], w_ref[...], preferred_element_type=F32) + b_ref[...]
    kv16_ref[...] = acc.astype(kv16_ref.dtype)

    def put(o_ref):
        for h in range(nh):
            o_ref[pl.ds(h, acc.shape[0], stride=nh), :] = acc[:, h * dh:(h + 1) * dh]

    for s in range(nseg):
        pl.when(jnp.logical_and(j == s, i < npt))(functools.partial(put, state_refs[s]))
        pl.when(jnp.logical_and(j == s, i >= npt))(functools.partial(put, state_refs[nseg + s]))


def _kv_proj(x, w, layer, b, bufs_p, bufs_s, np_rows, nh, dh):
    m, k = x.shape
    seg = nh * dh
    nseg = len(bufs_p)
    tm = _pick(math.gcd(np_rows, m - np_rows), (512, 256, 128, 64, 8))
    npt = np_rows // tm
    any_spec = pl.BlockSpec(memory_space=pl.ANY)
    p_spec = pl.BlockSpec((None, tm * nh, dh), lambda i, j: (layer, jnp.minimum(i, npt - 1), 0))
    s_spec = pl.BlockSpec((None, tm * nh, dh), lambda i, j: (layer, jnp.maximum(i - npt, 0), 0))
    bufs = tuple(bufs_p) + tuple(bufs_s)
    outs = pl.pallas_call(
        functools.partial(_kv_kernel, nseg=nseg, npt=npt, nh=nh, dh=dh),
        grid=(m // tm, nseg),
        in_specs=[pl.BlockSpec((tm, k), lambda i, j: (i, 0)),
                  pl.BlockSpec((None, k, seg), lambda i, j: (layer, 0, j + 1 + j // 2)),
                  pl.BlockSpec((1, seg), lambda i, j: (0, j))] + [any_spec] * (2 * nseg),
        out_specs=[pl.BlockSpec((None, tm, seg), lambda i, j: (j, i, 0))] + [p_spec] * nseg + [s_spec] * nseg,
        out_shape=[jax.ShapeDtypeStruct((nseg, m, seg), BF)]
                  + [jax.ShapeDtypeStruct(a.shape, a.dtype) for a in bufs],
        input_output_aliases={3 + t: 1 + t for t in range(2 * nseg)},
        compiler_params=_cp(("arbitrary", "arbitrary"),
                            2 * (tm * k * 2 + k * seg * 2 + tm * seg * 2 + 2 * nseg * tm * seg * 8) + 4 * MIB),
        name="kv_proj",
    )(x, w, b.reshape(1, nseg * seg).astype(F32), *bufs)
    return outs[0], outs[1:1 + nseg], outs[1 + nseg:]


def _attn_scratch(rows, tk, dv, nh=None):
    per_head = (lambda s: s) if nh is None else (lambda s: (nh,) + s)
    return [pltpu.VMEM((rows, tk), F32), pltpu.VMEM((rows, tk), BF), pltpu.VMEM(per_head((rows, LANE)), F32),
            pltpu.VMEM((rows, LANE), F32), pltpu.VMEM(per_head((rows, dv + LANE)), F32)]


TILE_SKIP, TILE_MASKED, TILE_FULL = 0, 1, 2


def _softmax_block(q, kb, vb, bias_fn, mask_fn, st, tile_kind=None):
    s_sc, p_sc, m_sc, a_sc, acc_sc = st
    rows, dv = q.shape[0], vb.shape[1]
    tk = kb.shape[0]
    rg = min(ATT_RG, rows)
    rc = min(ATT_RC, rg)
    e0 = jnp.where(lax.broadcasted_iota(jnp.int32, (tk, LANE), 1) == 0, 1.0, 0.0).astype(BF)
    vext = jnp.concatenate([vb, e0], axis=1)

    def kind(r0, j):
        if tile_kind is not None:
            return tile_kind(r0, j)
        return TILE_FULL if mask_fn is None else TILE_MASKED

    def scores(sl, r0, j):
        x = s_sc[sl, j * LANE:(j + 1) * LANE] + bias_fn(r0, j)
        return jnp.where(mask_fn(r0, j), x, NEG) if kind(r0, j) == TILE_MASKED else x

    for g in range(rows // rg):
        gs = slice(g * rg, (g + 1) * rg)
        chunks = [g * rg + c * rc for c in range(rg // rc)]
        live = {r0: [j for j in range(tk // LANE) if kind(r0, j) != TILE_SKIP] for r0 in chunks}
        ntk = LANE * (1 + max(max(js) for js in live.values()))
        s_sc[gs, :ntk] = lax.dot_general(q[gs], kb[:ntk], (((1,), (1,)), ((), ())), preferred_element_type=F32)
        for r0 in chunks:
            sl = slice(r0, r0 + rc)
            mv = None
            for j in live[r0]:
                x = scores(sl, r0, j)
                mv = x if mv is None else jnp.maximum(mv, x)
            m_old = m_sc[sl]
            m_new = jnp.maximum(m_old, jnp.broadcast_to(jnp.max(mv, axis=-1, keepdims=True), (rc, LANE)))
            a_sc[sl] = jnp.exp2(m_old - m_new)
            m_sc[sl] = m_new
        for r0 in chunks:
            sl = slice(r0, r0 + rc)
            m_new = m_sc[sl]
            for j in range(ntk // LANE):
                if j in live[r0]:
                    p = jnp.exp2(scores(sl, r0, j) - m_new).astype(BF)
                else:
                    p = jnp.zeros((rc, LANE), BF)
                p_sc[sl, j * LANE:(j + 1) * LANE] = p
        pv = jnp.dot(p_sc[gs, :ntk], vext[:ntk], preferred_element_type=F32)
        alpha = a_sc[gs]
        acc_sc[gs, :dv] = alpha * acc_sc[gs, :dv] + pv[:, :dv]
        acc_sc[gs, dv:] = alpha * acc_sc[gs, dv:] + pv[:, dv:]


def _state_init(st):
    _, _, m_sc, _, acc_sc = st
    m_sc[...] = jnp.full(m_sc.shape, NEG, F32)
    acc_sc[...] = jnp.zeros(acc_sc.shape, F32)


def _attn_out(acc, dv):
    return acc[:, :dv] / acc[:, dv:dv + 1]


def _diff_split_q(q, hd):
    lane = lax.broadcasted_iota(jnp.int32, q.shape, 1)
    zero = jnp.zeros_like(q)
    return jnp.concatenate([jnp.where(lane < hd, q, zero), jnp.where(lane >= hd, q, zero)], axis=0)


def _diff_bias_diag(slope2, qpos, kpos):
    bias = slope2 * (qpos - jnp.abs(qpos - kpos)).astype(F32)
    mask = (kpos >> CHUNK_SHIFT) <= (qpos >> CHUNK_SHIFT)
    return bias, mask


def _diff_finish(acc, lamp_ref, gain_ref, tq, lam_init):
    o = _attn_out(acc, gain_ref.shape[-1])
    lp = lamp_ref[...]
    lam = (jnp.exp(jnp.sum(lp[0:1] * lp[1:2], axis=-1, keepdims=True))
           - jnp.exp(jnp.sum(lp[2:3] * lp[3:4], axis=-1, keepdims=True)) + lam_init)
    od = o[:tq] - lam * o[tq:]
    return _rms(od, gain_ref[...]) * (1.0 - lam_init)


def _lane_tile(row, j):
    return row[:, j * LANE:(j + 1) * LANE]


def _chunk_rows(first):
    return first + lax.broadcasted_iota(jnp.int32, (ATT_RC, 1), 0)


def _diff_prompt_kernel(slopes_ref, lamp_ref, gain_ref, q_ref, k_ref, v_ref, o_ref, *st, tq, hd, lam_init):
    h = pl.program_id(1)
    qi = pl.program_id(2)
    slope2 = slopes_ref[h]
    qq = _diff_split_q(q_ref[...], hd)
    _state_init(st)
    col = lax.broadcasted_iota(jnp.int32, (1, tq), 1)

    def past_block(ki, c):
        off = pl.multiple_of(ki * tq, tq)
        bias = slope2 * (ki * tq + col).astype(F32)
        _softmax_block(qq, k_ref[pl.ds(off, tq), :], v_ref[pl.ds(off, tq), :],
                       lambda r0, j: _lane_tile(bias, j), None, st)
        return c

    lax.fori_loop(0, qi, past_block, 0)
    off = pl.multiple_of(qi * tq, tq)
    kpos = qi * tq + col

    def diag(r0, j):
        return _diff_bias_diag(slope2, _chunk_rows(qi * tq + r0 % tq), _lane_tile(kpos, j))

    def diag_kind(r0, j):
        lo, hi = r0 % tq, r0 % tq + ATT_RC - 1
        if (j * LANE) >> CHUNK_SHIFT > hi >> CHUNK_SHIFT:
            return TILE_SKIP
        return TILE_FULL if (j * LANE + LANE - 1) >> CHUNK_SHIFT <= lo >> CHUNK_SHIFT else TILE_MASKED

    _softmax_block(qq, k_ref[pl.ds(off, tq), :], v_ref[pl.ds(off, tq), :], lambda r0, j: diag(r0, j)[0],
                   lambda r0, j: diag(r0, j)[1], st, diag_kind)
    o_ref[...] = _diff_finish(st[4][...], lamp_ref, gain_ref, tq, lam_init).astype(o_ref.dtype)


def _fox_prompt_kernel(q_ref, k_ref, v_ref, nck_ref, o_ref, *st, tq):
    qi = pl.program_id(2)
    q = q_ref[...]
    _state_init(st)
    col = lax.broadcasted_iota(jnp.int32, (1, tq), 1)

    def block(ki, mask_fn, tile_kind):
        off = pl.multiple_of(ki * tq, tq)
        bias = nck_ref[:, pl.ds(off, tq)]
        _softmax_block(q, k_ref[pl.ds(off, tq), :], v_ref[pl.ds(off, tq), :],
                       lambda r0, j: _lane_tile(bias, j), mask_fn, st, tile_kind)

    def past_block(ki, c):
        block(ki, None, None)
        return c

    def diag_kind(r0, j):
        if j * LANE > r0 + ATT_RC - 1:
            return TILE_SKIP
        return TILE_FULL if j * LANE + LANE - 1 <= r0 else TILE_MASKED

    lax.fori_loop(0, qi, past_block, 0)
    block(qi, lambda r0, j: _lane_tile(col, j) <= _chunk_rows(r0), diag_kind)
    o_ref[...] = _attn_out(st[4][...], v_ref.shape[-1]).astype(o_ref.dtype)


def _prompt_attention(kind, q, kv, b, t, nh, dh, q_col0, k_seg, v_seg, extra, lam_init=None):
    tq = _pick(t, (512, 256, 128))
    nq = t // tq
    q_spec = pl.BlockSpec((tq, dh), lambda bi, h, qi: (bi * nq + qi, q_col0 + h))
    k_spec = pl.BlockSpec((None, t, dh), lambda bi, h, qi: (k_seg, bi, h))
    v_spec = pl.BlockSpec((None, t, dh), lambda bi, h, qi: (v_seg, bi, h))
    o_spec = pl.BlockSpec((tq, dh), lambda bi, h, qi: (bi * nq + qi, h))
    rows = 2 * tq if kind == "diff" else tq
    scratch = _attn_scratch(rows, tq, dh)
    vm = 2 * (2 * t * dh * 2 + 2 * tq * dh * 2) + rows * (6 * tq + 4 * LANE * 4) + 8 * ATT_RG * tq * 4 + 4 * MIB
    if kind == "diff":
        slopes, lamp, gain = extra
        kern = functools.partial(_diff_prompt_kernel, tq=tq, hd=dh // 2, lam_init=lam_init)
        in_specs = [pl.BlockSpec(memory_space=pltpu.SMEM),
                    pl.BlockSpec(lamp.shape, lambda bi, h, qi: (0, 0)),
                    pl.BlockSpec(gain.shape, lambda bi, h, qi: (0, 0)),
                    q_spec, k_spec, v_spec]
        args = (slopes, lamp, gain, q, kv, kv)
    else:
        (nck,) = extra
        kern = functools.partial(_fox_prompt_kernel, tq=tq)
        in_specs = [q_spec, k_spec, v_spec,
                    pl.BlockSpec((None, None, 1, t), lambda bi, h, qi: (bi, h, 0, 0))]
        args = (q, kv, kv, nck)
    return pl.pallas_call(
        kern,
        grid=(b, nh, nq),
        in_specs=in_specs,
        out_specs=o_spec,
        out_shape=jax.ShapeDtypeStruct((b * t, nh * dh), BF),
        scratch_shapes=scratch,
        compiler_params=_cp(("parallel", "parallel", "arbitrary"), vm),
        name=kind + "_prompt_attn",
    )(*args)


def _head_state(st, h):
    s_sc, p_sc, m_sc, a_sc, acc_sc = st
    return s_sc, p_sc, m_sc.at[h], a_sc, acc_sc.at[h]


def _head_rows(c_ref, h, tk, nh):
    return c_ref[pl.ds(h, tk, stride=nh), :].astype(BF)


def _pad_keys(x, tkn):
    return x if x.shape[0] == tkn else jnp.concatenate([x, jnp.zeros((tkn - x.shape[0], x.shape[1]), x.dtype)], 0)


def _diff_sample_kernel(slopes_ref, lamp_ref, gain_ref, q_ref, kn_ref, vn_ref, kp_ref, vp_ref, o_ref, *st,
                        tq, tk, tkn, past, nh, dh, lam_init):
    kb_i = pl.program_id(1)

    @pl.when(kb_i == 0)
    def _():
        _state_init(st)

    q = q_ref[...]
    qqs = [_diff_split_q(q[:, h * dh:(h + 1) * dh], dh // 2) for h in range(nh)]
    kposf = (kb_i * tk + lax.broadcasted_iota(jnp.int32, (1, tk), 1)).astype(F32)
    for h in range(nh):
        bias = slopes_ref[h] * kposf
        _softmax_block(qqs[h], _head_rows(kp_ref, h, tk, nh), _head_rows(vp_ref, h, tk, nh),
                       lambda r0, j: _lane_tile(bias, j), None, _head_state(st, h))

    @pl.when(kb_i == pl.num_programs(1) - 1)
    def _():
        col = lax.broadcasted_iota(jnp.int32, (1, tkn), 1)
        for h in range(nh):
            hs = slice(h * dh, (h + 1) * dh)

            def diag(r0, j):
                cj = _lane_tile(col, j)
                bias, mask = _diff_bias_diag(slopes_ref[h], _chunk_rows(past + r0 % tq), past + cj)
                return bias, jnp.logical_and(mask, cj < tq)

            _softmax_block(qqs[h], _pad_keys(kn_ref[:, hs], tkn), _pad_keys(vn_ref[:, hs], tkn),
                           lambda r0, j: diag(r0, j)[0], lambda r0, j: diag(r0, j)[1], _head_state(st, h))
            o_ref[:, hs] = _diff_finish(st[4][h], lamp_ref, gain_ref, tq, lam_init).astype(o_ref.dtype)


def _fox_sample_kernel(q_ref, kn_ref, vn_ref, kp_ref, vp_ref, nckp_ref, nckn_ref, o_ref, *st, tq, tk, tkn, nh, dh):
    kb_i = pl.program_id(1)

    @pl.when(kb_i == 0)
    def _():
        _state_init(st)

    q = q_ref[...]
    for h in range(nh):
        bias = nckp_ref[h]
        _softmax_block(q[:, h * dh:(h + 1) * dh], _head_rows(kp_ref, h, tk, nh), _head_rows(vp_ref, h, tk, nh),
                       lambda r0, j: _lane_tile(bias, j), None, _head_state(st, h))

    @pl.when(kb_i == pl.num_programs(1) - 1)
    def _():
        col = lax.broadcasted_iota(jnp.int32, (1, tkn), 1)
        for h in range(nh):
            hs = slice(h * dh, (h + 1) * dh)
            bias = nckn_ref[h]
            _softmax_block(q[:, hs], _pad_keys(kn_ref[:, hs], tkn), _pad_keys(vn_ref[:, hs], tkn),
                           lambda r0, j: _lane_tile(bias, j),
                           lambda r0, j: _lane_tile(col, j) <= _chunk_rows(r0), _head_state(st, h))
            o_ref[:, hs] = _attn_out(st[4][h], dh).astype(o_ref.dtype)


def _sample_attention(kind, q, kv, cache_k, cache_v, layer, row0, b, t, nh, dh, q_col0, k_seg, v_seg, extra,
                      lam_init=None):
    past = cache_k.shape[2]
    cache_k, cache_v = (c.reshape(c.shape[0], c.shape[1], past * nh, dh) for c in (cache_k, cache_v))
    tk = _pick(past, (1024, 512, 256, 128))
    nkb = past // tk
    rb = row0 // t
    w = nh * dh
    qcb = q_col0 * dh // w
    q_spec = pl.BlockSpec((t, w), lambda bi, kb: (rb + bi, qcb))
    kn_spec = pl.BlockSpec((None, t, w), lambda bi, kb: (k_seg, rb + bi, 0))
    vn_spec = pl.BlockSpec((None, t, w), lambda bi, kb: (v_seg, rb + bi, 0))
    kp_spec = pl.BlockSpec((None, None, tk * nh, dh), lambda bi, kb: (layer, bi, kb, 0))
    o_spec = pl.BlockSpec((t, w), lambda bi, kb: (bi, 0))
    rows = 2 * t if kind == "diff" else t
    tkn = -(-t // LANE) * LANE
    scratch = _attn_scratch(rows, max(tk, tkn), dh, nh)
    vm = 2 * (2 * tk * nh * 2 * dh * 4 + 4 * t * w * 2) + 4 * nh * rows * LANE * 4 + 16 * rows * tk * 4 + 8 * MIB
    if kind == "diff":
        slopes, lamp, gain = extra
        kern = functools.partial(_diff_sample_kernel, tq=t, tk=tk, tkn=tkn, past=past, nh=nh, dh=dh,
                                 lam_init=lam_init)
        in_specs = [pl.BlockSpec(memory_space=pltpu.SMEM),
                    pl.BlockSpec(lamp.shape, lambda bi, kb: (0, 0)),
                    pl.BlockSpec(gain.shape, lambda bi, kb: (0, 0)),
                    q_spec, kn_spec, vn_spec, kp_spec, kp_spec]
        args = (slopes, lamp, gain, q, kv, kv, cache_k, cache_v)
    else:
        nckp, nckn = extra
        nckn = jnp.pad(nckn, ((0, 0), (0, 0), (0, 0), (0, tkn - t)))
        kern = functools.partial(_fox_sample_kernel, tq=t, tk=tk, tkn=tkn, nh=nh, dh=dh)
        in_specs = [q_spec, kn_spec, vn_spec, kp_spec, kp_spec,
                    pl.BlockSpec((None, nh, 1, tk), lambda bi, kb: (bi, 0, 0, kb)),
                    pl.BlockSpec((None, nh, 1, tkn), lambda bi, kb: (bi, 0, 0, 0))]
        args = (q, kv, kv, cache_k, cache_v, nckp, nckn)
    return pl.pallas_call(
        kern,
        grid=(b, nkb),
        in_specs=in_specs,
        out_specs=o_spec,
        out_shape=jax.ShapeDtypeStruct((b * t, w), BF),
        scratch_shapes=scratch,
        compiler_params=_cp(("parallel", "arbitrary"), vm),
        name=kind + "_sample_attn",
    )(*args)


def _expm1(x):
    u = jnp.exp(x)
    um1 = u - 1.0
    return jnp.where(um1 == 0.0, x, jnp.where(um1 == -1.0, -1.0, um1 * x / jnp.log(u)))


def _rglru_kernel(rx_ref, rg_ref, hist_ref, h0_ref, cw_ref, cb_ref, wa_ref, wi_ref, ba_ref, bi_ref, lam_ref,
                  yc_ref, hl_ref, cn_ref, prev_sc, h_sc, *, tt, cwidth):
    j = pl.program_id(2)

    @pl.when(j == 0)
    def _():
        prev_sc[...] = hist_ref[...]
        h_sc[...] = h0_ref[...]

    x = rx_ref[...]
    xp = jnp.concatenate([prev_sc[...], x], axis=0)
    cw = cw_ref[...]
    xc = cb_ref[...]
    for jj in range(cwidth):
        s0 = 8 - (cwidth - 1) + jj
        xc = xc + xp[s0:s0 + tt] * cw[jj:jj + 1]
    xcb = xc.astype(BF)
    r = jax.nn.sigmoid(jnp.dot(xcb, wa_ref[...], preferred_element_type=F32) + ba_ref[...])
    ig = jax.nn.sigmoid(jnp.dot(xcb, wi_ref[...], preferred_element_type=F32) + bi_ref[...])
    log_a = -LRU_C * r * jax.nn.softplus(-lam_ref[...])
    a = jnp.exp(log_a)
    bb = jnp.sqrt(-_expm1(2.0 * log_a)) * (ig * xc)

    row = lax.broadcasted_iota(jnp.int32, (tt, 1), 0)
    d = 1
    while d < tt:
        keep = row >= d
        a_sh = jnp.where(keep, pltpu.roll(a, d, 0), 1.0)
        b_sh = jnp.where(keep, pltpu.roll(bb, d, 0), 0.0)
        bb = a * b_sh + bb
        a = a * a_sh
        d *= 2
    hseq = bb + a * h_sc[...]

    yc_ref[...] = (hseq * rg_ref[...].astype(F32)).astype(yc_ref.dtype)
    h_sc[...] = hseq[tt - 1:tt]
    prev_sc[...] = xp[tt:tt + 8]

    @pl.when(j == pl.num_programs(2) - 1)
    def _():
        hl_ref[...] = hseq[tt - 1:tt]
        cn_ref[...] = xp[tt:tt + 8]


def _rglru(rx, rgg, hist8, h0, prm, layer, row0, b, t):
    c = rx.shape[1]
    cw, cb, wa, wi, ba, bi, lam = prm
    ct = wa.shape[2]
    nc = c // ct
    tt = _pick(t, (256, 128, 64, 32, 16, 8))
    nt = t // tt
    rb = row0 // tt
    cwidth = cw.shape[0]
    vec = lambda: pl.BlockSpec((1, ct), lambda bi_, ci, j: (0, ci))
    row_in = pl.BlockSpec((tt, ct), lambda bi_, ci, j: (rb + bi_ * nt + j, ci))
    wsp = pl.BlockSpec((None, None, ct, ct), lambda bi_, ci, j: (layer, ci, 0, 0))
    return pl.pallas_call(
        functools.partial(_rglru_kernel, tt=tt, cwidth=cwidth),
        grid=(b, nc, nt),
        in_specs=[row_in, row_in,
                  pl.BlockSpec((None, 8, ct), lambda bi_, ci, j: (bi_, 0, ci)),
                  pl.BlockSpec((None, 1, ct), lambda bi_, ci, j: (bi_, 0, ci)),
                  pl.BlockSpec((cwidth, ct), lambda bi_, ci, j: (0, ci)),
                  vec(), wsp, wsp, vec(), vec(), vec()],
        out_specs=[pl.BlockSpec((tt, ct), lambda bi_, ci, j: (bi_ * nt + j, ci)),
                   pl.BlockSpec((None, 1, ct), lambda bi_, ci, j: (bi_, 0, ci)),
                   pl.BlockSpec((None, 8, ct), lambda bi_, ci, j: (bi_, 0, ci))],
        out_shape=[jax.ShapeDtypeStruct((b * t, c), BF),
                   jax.ShapeDtypeStruct((b, 1, c), F32),
                   jax.ShapeDtypeStruct((b, 8, c), F32)],
        scratch_shapes=[pltpu.VMEM((8, ct), F32), pltpu.VMEM((1, ct), F32)],
        compiler_params=_cp(("parallel", "parallel", "arbitrary"), 32 * tt * ct * 4 + 8 * MIB),
        name="rglru",
    )(rx, rgg, hist8, h0, cw, cb.reshape(1, c), wa, wi, ba.reshape(1, c), bi.reshape(1, c), lam.reshape(1, c))


def _merge_kernel(x_ref, *refs, npt):
    y_refs, (wg_refs, bg_refs, wp_refs), o_ref = refs[:6], (refs[6:9], refs[9:12], refs[12:15]), refs[15]
    is_prompt = pl.program_id(0) < npt
    x = x_ref[...]
    out = None
    for k in range(3):
        y = jnp.where(is_prompt, y_refs[2 * k][...], y_refs[2 * k + 1][...])
        gate = jax.nn.sigmoid(jnp.dot(x, wg_refs[k][...], preferred_element_type=F32) + bg_refs[k][...])
        term = gate * jnp.dot(y, wp_refs[k][...], preferred_element_type=F32)
        out = term if out is None else out + term
    o_ref[...] = out.astype(o_ref.dtype)


def _merge(x, branches, w_gate, gcol0, b_gate, wps, layer):
    n, d = x.shape
    np_rows = branches[0][0].shape[0]
    tm = _pick(math.gcd(np_rows, n - np_rows), (512, 256, 128, 64, 8))
    tn = _pick(d, (512, 256, 128))
    nd = d // tn
    npt = np_rows // tm
    y_specs, ys = [], []
    for yp, ys_ in branches:
        y_specs += [pl.BlockSpec((tm, yp.shape[1]), lambda i, j: (jnp.minimum(i, npt - 1), 0)),
                    pl.BlockSpec((tm, yp.shape[1]), lambda i, j: (jnp.maximum(i - npt, 0), 0))]
        ys += [yp, ys_]
    wg_specs = [pl.BlockSpec((None, d, tn), lambda i, j, k=k: (layer, 0, gcol0 + k * nd + j)) for k in range(3)]
    bg_specs = [pl.BlockSpec((1, tn), lambda i, j, k=k: (0, k * nd + j)) for k in range(3)]
    wp_specs = [pl.BlockSpec((None, w.shape[1], tn), lambda i, j: (layer, 0, j)) for w in wps]
    bg = b_gate.reshape(1, 3 * d)
    return pl.pallas_call(
        functools.partial(_merge_kernel, npt=npt),
        grid=(n // tm, nd),
        in_specs=[pl.BlockSpec((tm, d), lambda i, j: (i, 0))] + y_specs + wg_specs + bg_specs + wp_specs,
        out_specs=pl.BlockSpec((tm, tn), lambda i, j: (i, j)),
        out_shape=jax.ShapeDtypeStruct((n, d), BF),
        compiler_params=_cp(("parallel", "parallel"), 2 * (tm * d * 2 + 3 * d * tn * 2 + 12 * tm * tn * 2) + 16 * MIB),
        name="merge",
    )(x, *ys, w_gate, w_gate, w_gate, bg, bg, bg, *wps)


def _mm_res_norm_kernel(x_ref, w_ref, h_ref, g_ref, ho_ref, xn_ref):
    hn = h_ref[...] + jnp.dot(x_ref[...], w_ref[...], preferred_element_type=F32)
    ho_ref[...] = hn
    xn_ref[...] = _rms(hn, g_ref[...]).astype(xn_ref.dtype)


def _mm_res_norm(x, w, layer, h, g, xn_dtype):
    n, k = x.shape
    d = w.shape[2]
    tm = _pick(n, (256, 128, 64, 8))
    row = lambda c: pl.BlockSpec((tm, c), lambda i: (i, 0))
    return pl.pallas_call(
        _mm_res_norm_kernel,
        grid=(n // tm,),
        in_specs=[row(k), pl.BlockSpec((None, k, d), lambda i: (layer, 0, 0)), row(d),
                  pl.BlockSpec((1, d), lambda i: (0, 0))],
        out_specs=[row(d), row(d)],
        out_shape=[jax.ShapeDtypeStruct((n, d), F32), jax.ShapeDtypeStruct((n, d), xn_dtype)],
        compiler_params=_cp(("parallel",), 2 * (k * d * 2 + tm * k * 2 + 3 * tm * d * 4) + 8 * MIB),
        name="out_proj",
    )(x, w, h, g.reshape(1, d))


def _ffn_kernel(x_ref, wg_ref, wu_ref, wd_ref, h_ref, g_ref, ho_ref, xn_ref, acc_sc):
    f = pl.program_id(1)

    @pl.when(f == 0)
    def _():
        acc_sc[...] = jnp.zeros_like(acc_sc)

    x = x_ref[...]
    a = (jax.nn.silu(jnp.dot(x, wg_ref[...], preferred_element_type=F32))
         * jnp.dot(x, wu_ref[...], preferred_element_type=F32)).astype(BF)
    acc_sc[...] += jnp.dot(a, wd_ref[...], preferred_element_type=F32)

    @pl.when(f == pl.num_programs(1) - 1)
    def _():
        hn = h_ref[...] + acc_sc[...]
        ho_ref[...] = hn
        xn_ref[...] = _rms(hn, g_ref[...]).astype(xn_ref.dtype)


def _ffn(x, wg, wu, wd, layer, h, g):
    n, d = x.shape
    ff = wg.shape[2]
    tm = _pick(n, (512, 256, 128, 64, 8))
    tf = _pick(ff, (512, 256, 128))
    row = pl.BlockSpec((tm, d), lambda i, f: (i, 0))
    vm = 2 * (3 * d * tf * 2 + tm * d * (2 + 4 + 4 + 2)) + tm * d * 4 + 3 * tm * tf * 4 + 4 * MIB
    return pl.pallas_call(
        _ffn_kernel,
        grid=(n // tm, ff // tf),
        in_specs=[row,
                  pl.BlockSpec((None, d, tf), lambda i, f: (layer, 0, f)),
                  pl.BlockSpec((None, d, tf), lambda i, f: (layer, 0, f)),
                  pl.BlockSpec((None, tf, d), lambda i, f: (layer, f, 0)),
                  row, pl.BlockSpec((1, d), lambda i, f: (0, 0))],
        out_specs=[row, row],
        out_shape=[jax.ShapeDtypeStruct((n, d), F32), jax.ShapeDtypeStruct((n, d), BF)],
        scratch_shapes=[pltpu.VMEM((tm, d), F32)],
        compiler_params=_cp(("parallel", "arbitrary"), vm),
        name="ffn",
    )(x, wg, wu, wd, h, g.reshape(1, d))


def _router_kernel(x_ref, wh_ref, wl_ref, b_ref, e_ref, p_ref, *, n_exp):
    x = x_ref[...]
    xh = x.astype(BF)
    xl = (x - xh.astype(F32)).astype(BF)
    wh = wh_ref[...]
    logits = (jnp.dot(xh, wh, preferred_element_type=F32) + jnp.dot(xl, wh, preferred_element_type=F32)
              + jnp.dot(xh, wl_ref[...], preferred_element_type=F32) + b_ref[...])
    lane = lax.broadcasted_iota(jnp.int32, logits.shape, 1)
    logits = jnp.where(lane < n_exp, logits, NEG)
    m1 = jnp.max(logits, axis=-1, keepdims=True)
    i1 = jnp.min(jnp.where(logits == m1, lane, LANE), axis=-1, keepdims=True)
    rest = jnp.where(lane == i1, NEG, logits)
    m2 = jnp.max(rest, axis=-1, keepdims=True)
    i2 = jnp.min(jnp.where(rest == m2, lane, LANE), axis=-1, keepdims=True)
    e2 = jnp.exp(m2 - m1)
    p1 = 1.0 / (1.0 + e2)
    p2 = e2 / (1.0 + e2)
    e_ref[...] = jnp.where(lane == 0, i1, jnp.where(lane == 1, i2, 0))
    p_ref[...] = jnp.where(lane == 0, p1, jnp.where(lane == 1, p2, 0.0))


def _router(x, wr, br):
    n, d = x.shape
    n_exp = wr.shape[1]
    tm = _pick(n, (256, 128, 64, 8))
    wpad = jnp.zeros((d, LANE), F32).at[:, :n_exp].set(wr)
    wh = wpad.astype(BF)
    wl = (wpad - wh.astype(F32)).astype(BF)
    bpad = jnp.zeros((1, LANE), F32).at[0, :n_exp].set(br)
    row = lambda: pl.BlockSpec((tm, LANE), lambda i: (i, 0))
    return pl.pallas_call(
        functools.partial(_router_kernel, n_exp=n_exp),
        grid=(n // tm,),
        in_specs=[pl.BlockSpec((tm, d), lambda i: (i, 0)),
                  pl.BlockSpec((d, LANE), lambda i: (0, 0)),
                  pl.BlockSpec((d, LANE), lambda i: (0, 0)),
                  pl.BlockSpec((1, LANE), lambda i: (0, 0))],
        out_specs=[row(), row()],
        out_shape=[jax.ShapeDtypeStruct((n, LANE), jnp.int32), jax.ShapeDtypeStruct((n, LANE), F32)],
        compiler_params=_cp(("parallel",), 16 * MIB),
        name="moe_router",
    )(x, wh, wl, bpad)


def _moe_ffn_kernel(te_ref, nu_ref, rt_ref, x_hbm, wg_ref, wu_ref, wd_ref, y_ref, xf_sc, xb_sc, acc_sc, sem,
                    *, tm):
    i = pl.program_id(0)
    f = pl.program_id(1)

    def row_copy(r, tok):
        return pltpu.make_async_copy(x_hbm.at[pl.ds(tok, 1)], xf_sc.at[pl.ds(r, 1)], sem.at[0])

    @pl.when(i < nu_ref[0])
    def _():
        @pl.when(f == 0)
        def _():
            def issue(r, c):
                row_copy(r, rt_ref[0, r]).start()
                return c

            def drain(r, c):
                row_copy(r, 0).wait()
                return c

            lax.fori_loop(0, tm, issue, 0, unroll=8)
            lax.fori_loop(0, tm, drain, 0, unroll=8)
            xb_sc[...] = xf_sc[...].astype(BF)
            acc_sc[...] = jnp.zeros_like(acc_sc)

        x = xb_sc[...]
        a = (jax.nn.silu(jnp.dot(x, wg_ref[...], preferred_element_type=F32))
             * jnp.dot(x, wu_ref[...], preferred_element_type=F32)).astype(BF)
        acc_sc[...] += jnp.dot(a, wd_ref[...], preferred_element_type=F32)

        @pl.when(f == pl.num_programs(1) - 1)
        def _():
            y_ref[...] = acc_sc[...]

    @pl.when(jnp.logical_and(i >= nu_ref[0], f == pl.num_programs(1) - 1))
    def _():
        y_ref[...] = jnp.zeros_like(y_ref)


def _moe_ffn(x, tile_expert, row_token, n_used, wg, wu, wd, layer, tm):
    d = x.shape[1]
    ff = wg.shape[3]
    nt = tile_expert.shape[0]
    tf = _pick(ff, (256, 128))
    nf = ff // tf

    def tile(i, nu):
        return jnp.minimum(i, nu[0] - 1)

    def fidx(i, f, nu):
        return jnp.where(i < nu[0], f, nf - 1)

    grid_spec = pltpu.PrefetchScalarGridSpec(
        num_scalar_prefetch=2,
        grid=(nt, nf),
        in_specs=[pl.BlockSpec((None, 1, tm), lambda i, f, te, nu: (i, 0, 0), memory_space=pltpu.SMEM),
                  pl.BlockSpec(memory_space=pl.ANY),
                  pl.BlockSpec((None, None, d, tf),
                               lambda i, f, te, nu: (layer, te[tile(i, nu)], 0, fidx(i, f, nu))),
                  pl.BlockSpec((None, None, d, tf),
                               lambda i, f, te, nu: (layer, te[tile(i, nu)], 0, fidx(i, f, nu))),
                  pl.BlockSpec((None, None, tf, d),
                               lambda i, f, te, nu: (layer, te[tile(i, nu)], fidx(i, f, nu), 0))],
        out_specs=pl.BlockSpec((tm, d), lambda i, f, te, nu: (i, 0)),
        scratch_shapes=[pltpu.VMEM((tm, d), F32), pltpu.VMEM((tm, d), BF), pltpu.VMEM((tm, d), F32),
                        pltpu.SemaphoreType.DMA((1,))],
    )
    vm = 2 * (3 * d * tf * 2 + tm * d * 4) + tm * d * 10 + 3 * tm * tf * 4 + 4 * MIB
    return pl.pallas_call(
        functools.partial(_moe_ffn_kernel, tm=tm),
        grid_spec=grid_spec,
        out_shape=jax.ShapeDtypeStruct((nt * tm, d), F32),
        compiler_params=_cp(("arbitrary", "arbitrary"), vm),
        name="moe_ffn",
    )(tile_expert, n_used, row_token.reshape(nt, 1, tm), x, wg, wu, wd)


def _moe_combine_kernel(pos_ref, y_hbm, p_ref, h_ref, g_ref, ho_ref, xn_ref, y_sc, sem, *, tm):
    def row_copy(k, r, src):
        return pltpu.make_async_copy(y_hbm.at[pl.ds(src, 1)], y_sc.at[k, pl.ds(r, 1)], sem.at[0])

    def issue(r, c):
        row_copy(0, r, pos_ref[0, 2 * r]).start()
        row_copy(1, r, pos_ref[0, 2 * r + 1]).start()
        return c

    def drain(r, c):
        row_copy(0, r, 0).wait()
        row_copy(1, r, 0).wait()
        return c

    lax.fori_loop(0, tm, issue, 0, unroll=8)
    lax.fori_loop(0, tm, drain, 0, unroll=8)
    p = p_ref[...]
    hn = h_ref[...] + (p[:, 0:1] * y_sc[0] + p[:, 1:2] * y_sc[1])
    ho_ref[...] = hn
    xn_ref[...] = _rms(hn, g_ref[...]).astype(xn_ref.dtype)


def _moe_combine(pos, y, p, h, g):
    n, d = h.shape
    tm = _pick(n, (256, 128, 64, 8))
    row = lambda c: pl.BlockSpec((tm, c), lambda i: (i, 0))
    return pl.pallas_call(
        functools.partial(_moe_combine_kernel, tm=tm),
        grid=(n // tm,),
        in_specs=[pl.BlockSpec((None, 1, 2 * tm), lambda i: (i, 0, 0), memory_space=pltpu.SMEM),
                  pl.BlockSpec(memory_space=pl.ANY), row(LANE), row(d),
                  pl.BlockSpec((1, d), lambda i: (0, 0))],
        out_specs=[row(d), row(d)],
        scratch_shapes=[pltpu.VMEM((2, tm, d), F32), pltpu.SemaphoreType.DMA((1,))],
        out_shape=[jax.ShapeDtypeStruct((n, d), F32), jax.ShapeDtypeStruct((n, d), BF)],
        compiler_params=_cp(("arbitrary",), 2 * (tm * d * 10 + tm * LANE * 4) + 2 * tm * d * 4 + 8 * MIB),
        name="moe_combine",
    )(pos.reshape(n // tm, 1, 2 * tm), y, p, h, g.reshape(1, d))


def _moe(xn, h, wr, br, wg, wu, wd, layer, g_next):
    n = xn.shape[0]
    n_exp = wr.shape[1]
    tm = _pick(n, (512, 256, 128, 64, 8))
    e_pad, p_pad = _router(xn, wr, br)
    top_e = e_pad[:, :2]
    onehot = (top_e[:, :, None] == jnp.arange(n_exp, dtype=jnp.int32)).astype(jnp.int32).sum(axis=1)
    counts = onehot.sum(axis=0)
    padded = ((counts + tm - 1) // tm) * tm
    ends = jnp.cumsum(padded)
    starts = ends - padded
    rank = jnp.cumsum(onehot, axis=0) - onehot
    pos = starts[top_e] + jnp.take_along_axis(rank, top_e, axis=1)
    nt = (2 * n + n_exp * (tm - 1) + tm - 1) // tm
    row_token = jnp.zeros((nt * tm,), jnp.int32).at[pos.reshape(-1)].set(
        jnp.repeat(jnp.arange(n, dtype=jnp.int32), 2))
    tile_start = jnp.arange(nt, dtype=jnp.int32) * tm
    tile_expert = jnp.minimum((tile_start[:, None] >= ends[None, :]).astype(jnp.int32).sum(axis=1), n_exp - 1)
    n_used = (ends[-1:] // tm).astype(jnp.int32)
    y = _moe_ffn(xn, tile_expert, row_token, n_used, wg, wu, wd, layer, tm)
    return _moe_combine(pos.reshape(-1).astype(jnp.int32), y, p_pad, h, g_next)


def _ple_kernel(x_ref, p_ref, wg_ref, wu_ref, h_ref, g_ref, oa_ref, ob_ref, *, npt):
    gate = jax.nn.sigmoid(jnp.dot(x_ref[...], wg_ref[...], preferred_element_type=F32))
    up = jnp.dot(p_ref[...].astype(BF), wu_ref[...], preferred_element_type=F32)
    hn = h_ref[...] + gate * up
    xn = _rms(hn, g_ref[...])
    if npt is None:
        oa_ref[...] = hn
        ob_ref[...] = xn.astype(ob_ref.dtype)
    else:
        i = pl.program_id(0)

        @pl.when(i < npt)
        def _():
            oa_ref[...] = xn

        @pl.when(i >= npt)
        def _():
            ob_ref[...] = xn


def _ple(x, p, wg, wu, layer, h, g, np_rows=None):
    n, d = x.shape
    dp = p.shape[1]
    row = lambda c: pl.BlockSpec((tm, c), lambda i: (i, 0))
    wsp = lambda a: pl.BlockSpec((None,) + a.shape[1:], lambda i: (layer, 0, 0))
    if np_rows is None:
        tm, npt = _pick(n, (256, 128, 64, 8)), None
        out_specs = [row(d), row(d)]
        out_shape = [jax.ShapeDtypeStruct((n, d), F32), jax.ShapeDtypeStruct((n, d), BF)]
    else:
        tm = _pick(math.gcd(np_rows, n - np_rows), (256, 128, 64, 8))
        npt = np_rows // tm
        out_specs = [pl.BlockSpec((tm, d), lambda i: (jnp.minimum(i, npt - 1), 0)),
                     pl.BlockSpec((tm, d), lambda i: (jnp.maximum(i - npt, 0), 0))]
        out_shape = [jax.ShapeDtypeStruct((np_rows, d), F32), jax.ShapeDtypeStruct((n - np_rows, d), F32)]
    return pl.pallas_call(
        functools.partial(_ple_kernel, npt=npt),
        grid=(n // tm,),
        in_specs=[row(d), row(dp), wsp(wg), wsp(wu), row(d), pl.BlockSpec((1, d), lambda i: (0, 0))],
        out_specs=out_specs,
        out_shape=out_shape,
        compiler_params=_cp(("arbitrary",), 2 * (d * d * 2 + dp * d * 2 + tm * d * 14 + tm * dp * 4) + 8 * MIB),
        name="ple",
    )(x, p, wg, wu, h, g.reshape(1, d))


def _block_diag(w, group):
    nl, nb, r, _ = w.shape
    eye = jnp.eye(group, dtype=w.dtype)
    return jnp.einsum("ab,lcaij->lcaibj", eye, w.reshape(nl, nb // group, group, r, r)).reshape(
        nl, nb // group, group * r, group * r)


def kernel(x_prompt, x_sample, cache_diff_k, cache_diff_v, cache_fox_k, cache_fox_v, cache_fox_logf, state_rnn_h, state_rnn_conv, p_prompt, p_sample, norm_mix, w_in, b_in, lam_q1, lam_k1, lam_q2, lam_k2, diff_gain, conv_w, conv_b, w_a, b_a, w_i, b_i, lru_lambda, w_pa, w_pb, w_pc, w_o, norm_ffn, ffn_wg, ffn_wu, ffn_wd, moe_wr, moe_br, moe_wg, moe_wu, moe_wd, ple_norm, ple_up, ple_gate, final_norm):
    bp, tp, d = x_prompt.shape
    bs, ts, _ = x_sample.shape
    depth = w_in.shape[0]
    past = cache_diff_k.shape[2]
    nh, dh = cache_diff_k.shape[3], cache_diff_k.shape[4]
    seg = nh * dh
    c_rnn = state_rnn_h.shape[2]
    cwidth = conv_w.shape[1]
    assert cache_diff_v.shape[3:] == cache_fox_k.shape[3:] == cache_fox_v.shape[3:] == (nh, dh) and dh == LANE
    assert past % (1 << CHUNK_SHIFT) == 0 and ts <= (1 << CHUNK_SHIFT) and cwidth - 1 <= 8
    np_, ns_ = bp * tp, bs * ts
    n = np_ + ns_
    assert np_ % ts == 0

    o_ff = 6 * seg
    o_rx = o_ff + nh
    assert w_in.shape[2] == o_rx + 2 * c_rnn + 3 * d

    def cols(a, lo, width):
        return lax.slice_in_dim(a, lo, lo + width, axis=a.ndim - 1)

    w_qkv = cols(w_in, 0, o_ff).astype(BF)
    w_rest = cols(w_in, o_rx, 2 * c_rnn + 3 * d).astype(BF)
    w_ff = jnp.pad(cols(w_in, o_ff, nh), ((0, 0), (0, 0), (0, LANE - nh))).astype(BF)
    w_pa16, w_pb16, w_pc16, w_o16 = (a.astype(BF) for a in (w_pa, w_pb, w_pc, w_o))
    ffn16 = tuple(a.astype(BF) for a in (ffn_wg, ffn_wu, ffn_wd))
    moe16 = tuple(a.astype(BF) for a in (moe_wg, moe_wu, moe_wd))
    ple_gate16, ple_up16 = ple_gate.astype(BF), ple_up.astype(BF)
    grp = (RNN_CT if c_rnn % RNN_CT == 0 else c_rnn) // w_a.shape[2]
    wa16, wi16 = _block_diag(w_a, grp).astype(BF), _block_diag(w_i, grp).astype(BF)

    slopes2 = jnp.asarray([LOG2E * 2.0 ** (-8.0 * (k + 1) / nh) for k in range(nh)], F32)
    q_scale = LOG2E * jnp.concatenate([jnp.full((seg,), (dh // 2) ** -0.5, F32), jnp.full((seg,), dh ** -0.5, F32)])

    w0 = 8 - (cwidth - 1)
    hist8_s = jnp.pad(state_rnn_conv, ((0, 0), (0, 0), (w0, 0), (0, 0)))
    hist8_p = jnp.zeros((bp, 8, c_rnn), F32)
    h0_p = jnp.zeros((bp, 1, c_rnn), F32)
    kv_p = tuple(jnp.zeros((depth, np_ * nh, dh), F32) for _ in range(4))
    kv_s = tuple(jnp.zeros((depth, ns_ * nh, dh), F32) for _ in range(4))

    h = jnp.concatenate([x_prompt.reshape(np_, d), x_sample.reshape(ns_, d)], axis=0)
    xn = _rmsnorm(h, norm_mix[0], BF)
    st_p, st_s = [], []
    y = None
    for i in range(depth):
        bi_l = b_in[i]
        b_q = jnp.concatenate([cols(bi_l, 0, seg), cols(bi_l, 3 * seg, seg)])
        q16 = _mm(xn, w_qkv, i, 0, 3 * (seg // _pick(2 * seg, (512, 256, 128))), b_q, BF, act="scale",
                  scale=q_scale)
        b_kv = jnp.concatenate([cols(bi_l, seg, 2 * seg), cols(bi_l, 4 * seg, 2 * seg)])
        kv16, kv_p, kv_s = _kv_proj(xn, w_qkv, i, b_kv, kv_p, kv_s, np_, nh, dh)
        b_ff = jnp.pad(cols(bi_l, o_ff, nh), (0, LANE - nh))
        logf = _mm(xn, w_ff, i, 0, 1, b_ff, F32, act="log_sigmoid")[:, :nh]
        rx = _mm(xn, w_rest, i, 0, 1, cols(bi_l, o_rx, c_rnn), F32)
        rgg = _mm(xn, w_rest, i, c_rnn // _pick(c_rnn, (512, 256, 128)), 1, cols(bi_l, o_rx + c_rnn, c_rnn), BF,
                  act="gelu")

        lam_init = 0.8 - 0.6 * math.exp(-0.3 * i)
        lamp = jnp.stack([lam_q1[i], lam_k1[i], lam_q2[i], lam_k2[i]])
        dextra = (slopes2, lamp, diff_gain[i].reshape(1, -1))
        ya_p = _prompt_attention("diff", q16, kv16, bp, tp, nh, dh, 0, 0, 1, dextra, lam_init)
        ya_s = _sample_attention("diff", q16, kv16, cache_diff_k, cache_diff_v, i, np_, bs, ts, nh, dh, 0, 0, 1,
                                 dextra, lam_init)
        logf_p = logf[:np_].reshape(bp, tp, nh)
        logf_s = logf[np_:].reshape(bs, ts, nh)
        cum_p = jnp.cumsum(logf_p, axis=1)
        cum_past = jnp.cumsum(cache_fox_logf[i].astype(F32), axis=1)
        cum_s = cum_past[:, -1:] + jnp.cumsum(logf_s, axis=1)
        as_rows = lambda c: (-LOG2E) * jnp.transpose(c, (0, 2, 1))[:, :, None, :]
        yb_p = _prompt_attention("fox", q16, kv16, bp, tp, nh, dh, nh, 2, 3, (as_rows(cum_p),))
        yb_s = _sample_attention("fox", q16, kv16, cache_fox_k, cache_fox_v, i, np_, bs, ts, nh, dh, nh, 2, 3,
                                 (as_rows(cum_past), as_rows(cum_s)))
        rprm = (conv_w[i], conv_b[i], wa16, wi16, b_a[i], b_i[i], lru_lambda[i])
        yc_p, hl_p, cn_p = _rglru(rx, rgg, hist8_p, h0_p, rprm, i, 0, bp, tp)
        yc_s, hl_s, cn_s = _rglru(rx, rgg, hist8_s[i], state_rnn_h[i][:, None, :], rprm, i, np_, bs, ts)
        merged = _merge(xn, ((ya_p, ya_s), (yb_p, yb_s), (yc_p, yc_s)), w_rest,
                        2 * c_rnn // _pick(d, (512, 256, 128)), cols(bi_l, o_rx + 2 * c_rnn, 3 * d),
                        (w_pa16, w_pb16, w_pc16), i)
        j = i // 2
        is_moe = i % 2 == 1
        h, xn2 = _mm_res_norm(merged, w_o16, i, h, norm_ffn[i], F32 if is_moe else BF)
        if is_moe:
            h, xn3 = _moe(xn2, h, moe_wr[j], moe_br[j], *moe16, j, ple_norm[i])
        else:
            h, xn3 = _ffn(xn2, *ffn16, j, h, ple_norm[i])
        p_cat = jnp.concatenate([p_prompt[i].reshape(np_, -1), p_sample[i].reshape(ns_, -1)], axis=0)
        if i == depth - 1:
            y = _ple(xn3, p_cat, ple_gate16, ple_up16, i, h, final_norm, np_)
        else:
            h, xn = _ple(xn3, p_cat, ple_gate16, ple_up16, i, h, norm_mix[i + 1])
        st_p.append((logf_p, hl_p[:, 0], cn_p[:, w0:]))
        st_s.append((logf_s, hl_s[:, 0], cn_s[:, w0:]))

    def small(sts):
        return tuple(jnp.stack([s[k] for s in sts]) for k in range(3))

    def rows5(a, b, t):
        return a.reshape(depth, b, t, nh, dh)

    return ((y[0].reshape(bp, tp, d), y[1].reshape(bs, ts, d))
            + tuple(rows5(a, bp, tp) for a in kv_p) + small(st_p)
            + tuple(rows5(a, bs, ts) for a in kv_s) + small(st_s))
```

```python
import functools
import math

import jax
import jax.numpy as jnp
from jax import lax
from jax.experimental import pallas as pl
from jax.experimental.pallas import tpu as pltpu

F32 = jnp.float32
BF = jnp.bfloat16

EPS = 1e-6
CHUNK_SHIFT = 6
LRU_C = 8.0
NEG = -1e30
LOG2E = 1.4426950408889634
MIB = 1024 * 1024
V7X_VMEM_CAP = 60 * MIB
LANE = 128
RNN_CT = 256
ATT_RG = 256
ATT_RC = 32


def _pick(n, cands):
    for c in cands:
        if n % c == 0:
            return c
    raise ValueError(f"no tile in {cands} divides {n}")


def _cp(sem, vmem_bytes):
    return pltpu.CompilerParams(dimension_semantics=sem,
                                vmem_limit_bytes=int(min(max(vmem_bytes, 16 * MIB), V7X_VMEM_CAP)))


def _rms(xf, g):
    return xf * lax.rsqrt(jnp.mean(xf * xf, axis=-1, keepdims=True) + EPS) * g


def _rmsnorm_kernel(x_ref, g_ref, o_ref):
    o_ref[...] = _rms(x_ref[...], g_ref[...]).astype(o_ref.dtype)


def _rmsnorm(x, g, out_dtype):
    n, d = x.shape
    tm = _pick(n, (512, 256, 128, 64, 8))
    return pl.pallas_call(
        _rmsnorm_kernel,
        grid=(n // tm,),
        in_specs=[pl.BlockSpec((tm, d), lambda i: (i, 0)), pl.BlockSpec((1, d), lambda i: (0, 0))],
        out_specs=pl.BlockSpec((tm, d), lambda i: (i, 0)),
        out_shape=jax.ShapeDtypeStruct((n, d), out_dtype),
        compiler_params=_cp(("parallel",), 6 * tm * d * 4),
        name="rmsnorm",
    )(x, g.reshape(1, d))


def _mm_kernel(x_ref, w_ref, b_ref, s_ref, o_ref, *, act):
    acc = jnp.dot(x_ref[...], w_ref[...], preferred_element_type=F32) + b_ref[...]
    if act == "sigmoid":
        acc = jax.nn.sigmoid(acc)
    elif act == "gelu":
        acc = jax.nn.gelu(acc)
    elif act == "log_sigmoid":
        acc = jax.nn.log_sigmoid(acc)
    elif act == "scale":
        acc = acc * s_ref[...]
    o_ref[...] = acc.astype(o_ref.dtype)


def _mm(x, w, layer, col0, stride, b, out_dtype, act=None, scale=None):
    m, k = x.shape
    n = b.shape[0]
    tm = _pick(m, (1024, 512, 256, 128, 64, 8))
    tn = _pick(n, (512, 256, 128))
    if scale is None:
        scale = jnp.ones((n,), F32)
    return pl.pallas_call(
        functools.partial(_mm_kernel, act=act),
        grid=(m // tm, n // tn),
        in_specs=[pl.BlockSpec((tm, k), lambda i, j: (i, 0)),
                  pl.BlockSpec((None, k, tn), lambda i, j: (layer, 0, col0 + stride * j)),
                  pl.BlockSpec((1, tn), lambda i, j: (0, j)),
                  pl.BlockSpec((1, tn), lambda i, j: (0, j))],
        out_specs=pl.BlockSpec((tm, tn), lambda i, j: (i, j)),
        out_shape=jax.ShapeDtypeStruct((m, n), out_dtype),
        compiler_params=_cp(("parallel", "parallel"),
                            2 * (tm * k * 2 + k * tn * 2 + tm * tn * 4) + 3 * tm * tn * 4),
        name="mm_" + (act or "lin"),
    )(x, w, b.reshape(1, n).astype(F32), scale.reshape(1, n).astype(F32))


def _kv_kernel(x_ref, w_ref, b_ref, *refs, nseg, npt, nh, dh):
    outs = refs[2 * nseg:]
    kv16_ref, state_refs = outs[0], outs[1:]
    i = pl.program_id(0)
    j = pl.program_id(1)
    acc = jnp.dot(x_ref[...], w_ref[...], preferred_element_type=F32) + b_ref[...]
    kv16_ref[...] = acc.astype(kv16_ref.dtype)

    def put(o_ref):
        for h in range(nh):
            o_ref[pl.ds(h, acc.shape[0], stride=nh), :] = acc[:, h * dh:(h + 1) * dh]

    for s in range(nseg):
        pl.when(jnp.logical_and(j == s, i < npt))(functools.partial(put, state_refs[s]))
        pl.when(jnp.logical_and(j == s, i >= npt))(functools.partial(put, state_refs[nseg + s]))


def _kv_proj(x, w, layer, b, bufs_p, bufs_s, np_rows, nh, dh):
    m, k = x.shape
    seg = nh * dh
    nseg = len(bufs_p)
    tm = _pick(math.gcd(np_rows, m - np_rows), (512, 256, 128, 64, 8))
    npt = np_rows // tm
    any_spec = pl.BlockSpec(memory_space=pl.ANY)
    p_spec = pl.BlockSpec((None, tm * nh, dh), lambda i, j: (layer, jnp.minimum(i, npt - 1), 0))
    s_spec = pl.BlockSpec((None, tm * nh, dh), lambda i, j: (layer, jnp.maximum(i - npt, 0), 0))
    bufs = tuple(bufs_p) + tuple(bufs_s)
    outs = pl.pallas_call(
        functools.partial(_kv_kernel, nseg=nseg, npt=npt, nh=nh, dh=dh),
        grid=(m // tm, nseg),
        in_specs=[pl.BlockSpec((tm, k), lambda i, j: (i, 0)),
                  pl.BlockSpec((None, k, seg), lambda i, j: (layer, 0, j + 1 + j // 2)),
                  pl.BlockSpec((1, seg), lambda i, j: (0, j))] + [any_spec] * (2 * nseg),
        out_specs=[pl.BlockSpec((None, tm, seg), lambda i, j: (j, i, 0))] + [p_spec] * nseg + [s_spec] * nseg,
        out_shape=[jax.ShapeDtypeStruct((nseg, m, seg), BF)]
                  + [jax.ShapeDtypeStruct(a.shape, a.dtype) for a in bufs],
        input_output_aliases={3 + t: 1 + t for t in range(2 * nseg)},
        compiler_params=_cp(("arbitrary", "arbitrary"),
                            2 * (tm * k * 2 + k * seg * 2 + tm * seg * 2 + 2 * nseg * tm * seg * 8) + 4 * MIB),
        name="kv_proj",
    )(x, w, b.reshape(1, nseg * seg).astype(F32), *bufs)
    return outs[0], outs[1:1 + nseg], outs[1 + nseg:]


def _attn_scratch(rows, tk, dv, nh=None):
    per_head = (lambda s: s) if nh is None else (lambda s: (nh,) + s)
    return [pltpu.VMEM((rows, tk), F32), pltpu.VMEM((rows, tk), BF), pltpu.VMEM(per_head((rows, LANE)), F32),
            pltpu.VMEM((rows, LANE), F32), pltpu.VMEM(per_head((rows, dv + LANE)), F32)]


TILE_SKIP, TILE_MASKED, TILE_FULL = 0, 1, 2


def _softmax_block(q, kb, vb, bias_fn, mask_fn, st, tile_kind=None):
    s_sc, p_sc, m_sc, a_sc, acc_sc = st
    rows, dv = q.shape[0], vb.shape[1]
    tk = kb.shape[0]
    rg = min(ATT_RG, rows)
    rc = min(ATT_RC, rg)
    e0 = jnp.where(lax.broadcasted_iota(jnp.int32, (tk, LANE), 1) == 0, 1.0, 0.0).astype(BF)
    vext = jnp.concatenate([vb, e0], axis=1)

    def kind(r0, j):
        if tile_kind is not None:
            return tile_kind(r0, j)
        return TILE_FULL if mask_fn is None else TILE_MASKED

    def scores(sl, r0, j):
        x = s_sc[sl, j * LANE:(j + 1) * LANE] + bias_fn(r0, j)
        return jnp.where(mask_fn(r0, j), x, NEG) if kind(r0, j) == TILE_MASKED else x

    for g in range(rows // rg):
        gs = slice(g * rg, (g + 1) * rg)
        chunks = [g * rg + c * rc for c in range(rg // rc)]
        live = {r0: [j for j in range(tk // LANE) if kind(r0, j) != TILE_SKIP] for r0 in chunks}
        ntk = LANE * (1 + max(max(js) for js in live.values()))
        s_sc[gs, :ntk] = lax.dot_general(q[gs], kb[:ntk], (((1,), (1,)), ((), ())), preferred_element_type=F32)
        for r0 in chunks:
            sl = slice(r0, r0 + rc)
            mv = None
            for j in live[r0]:
                x = scores(sl, r0, j)
                mv = x if mv is None else jnp.maximum(mv, x)
            m_old = m_sc[sl]
            m_new = jnp.maximum(m_old, jnp.broadcast_to(jnp.max(mv, axis=-1, keepdims=True), (rc, LANE)))
            a_sc[sl] = jnp.exp2(m_old - m_new)
            m_sc[sl] = m_new
        for r0 in chunks:
            sl = slice(r0, r0 + rc)
            m_new = m_sc[sl]
            for j in range(ntk // LANE):
                if j in live[r0]:
                    p = jnp.exp2(scores(sl, r0, j) - m_new).astype(BF)
                else:
                    p = jnp.zeros((rc, LANE), BF)
                p_sc[sl, j * LANE:(j + 1) * LANE] = p
        pv = jnp.dot(p_sc[gs, :ntk], vext[:ntk], preferred_element_type=F32)
        alpha = a_sc[gs]
        acc_sc[gs, :dv] = alpha * acc_sc[gs, :dv] + pv[:, :dv]
        acc_sc[gs, dv:] = alpha * acc_sc[gs, dv:] + pv[:, dv:]


def _state_init(st):
    _, _, m_sc, _, acc_sc = st
    m_sc[...] = jnp.full(m_sc.shape, NEG, F32)
    acc_sc[...] = jnp.zeros(acc_sc.shape, F32)


def _attn_out(acc, dv):
    return acc[:, :dv] / acc[:, dv:dv + 1]


def _diff_split_q(q, hd):
    lane = lax.broadcasted_iota(jnp.int32, q.shape, 1)
    zero = jnp.zeros_like(q)
    return jnp.concatenate([jnp.where(lane < hd, q, zero), jnp.where(lane >= hd, q, zero)], axis=0)


def _diff_bias_diag(slope2, qpos, kpos):
    bias = slope2 * (qpos - jnp.abs(qpos - kpos)).astype(F32)
    mask = (kpos >> CHUNK_SHIFT) <= (qpos >> CHUNK_SHIFT)
    return bias, mask


def _diff_finish(acc, lamp_ref, gain_ref, tq, lam_init):
    o = _attn_out(acc, gain_ref.shape[-1])
    lp = lamp_ref[...]
    lam = (jnp.exp(jnp.sum(lp[0:1] * lp[1:2], axis=-1, keepdims=True))
           - jnp.exp(jnp.sum(lp[2:3] * lp[3:4], axis=-1, keepdims=True)) + lam_init)
    od = o[:tq] - lam * o[tq:]
    return _rms(od, gain_ref[...]) * (1.0 - lam_init)


def _lane_tile(row, j):
    return row[:, j * LANE:(j + 1) * LANE]


def _chunk_rows(first):
    return first + lax.broadcasted_iota(jnp.int32, (ATT_RC, 1), 0)


def _diff_prompt_kernel(slopes_ref, lamp_ref, gain_ref, q_ref, k_ref, v_ref, o_ref, *st, tq, hd, lam_init):
    h = pl.program_id(1)
    qi = pl.program_id(2)
    slope2 = slopes_ref[h]
    qq = _diff_split_q(q_ref[...], hd)
    _state_init(st)
    col = lax.broadcasted_iota(jnp.int32, (1, tq), 1)

    def past_block(ki, c):
        off = pl.multiple_of(ki * tq, tq)
        bias = slope2 * (ki * tq + col).astype(F32)
        _softmax_block(qq, k_ref[pl.ds(off, tq), :], v_ref[pl.ds(off, tq), :],
                       lambda r0, j: _lane_tile(bias, j), None, st)
        return c

    lax.fori_loop(0, qi, past_block, 0)
    off = pl.multiple_of(qi * tq, tq)
    kpos = qi * tq + col

    def diag(r0, j):
        return _diff_bias_diag(slope2, _chunk_rows(qi * tq + r0 % tq), _lane_tile(kpos, j))

    def diag_kind(r0, j):
        lo, hi = r0 % tq, r0 % tq + ATT_RC - 1
        if (j * LANE) >> CHUNK_SHIFT > hi >> CHUNK_SHIFT:
            return TILE_SKIP
        return TILE_FULL if (j * LANE + LANE - 1) >> CHUNK_SHIFT <= lo >> CHUNK_SHIFT else TILE_MASKED

    _softmax_block(qq, k_ref[pl.ds(off, tq), :], v_ref[pl.ds(off, tq), :], lambda r0, j: diag(r0, j)[0],
                   lambda r0, j: diag(r0, j)[1], st, diag_kind)
    o_ref[...] = _diff_finish(st[4][...], lamp_ref, gain_ref, tq, lam_init).astype(o_ref.dtype)


def _fox_prompt_kernel(q_ref, k_ref, v_ref, nck_ref, o_ref, *st, tq):
    qi = pl.program_id(2)
    q = q_ref[...]
    _state_init(st)
    col = lax.broadcasted_iota(jnp.int32, (1, tq), 1)

    def block(ki, mask_fn, tile_kind):
        off = pl.multiple_of(ki * tq, tq)
        bias = nck_ref[:, pl.ds(off, tq)]
        _softmax_block(q, k_ref[pl.ds(off, tq), :], v_ref[pl.ds(off, tq), :],
                       lambda r0, j: _lane_tile(bias, j), mask_fn, st, tile_kind)

    def past_block(ki, c):
        block(ki, None, None)
        return c

    def diag_kind(r0, j):
        if j * LANE > r0 + ATT_RC - 1:
            return TILE_SKIP
        return TILE_FULL if j * LANE + LANE - 1 <= r0 else TILE_MASKED

    lax.fori_loop(0, qi, past_block, 0)
    block(qi, lambda r0, j: _lane_tile(col, j) <= _chunk_rows(r0), diag_kind)
    o_ref[...] = _attn_out(st[4][...], v_ref.shape[-1]).astype(o_ref.dtype)


def _prompt_attention(kind, q, kv, b, t, nh, dh, q_col0, k_seg, v_seg, extra, lam_init=None):
    tq = _pick(t, (1024, 512, 256, 128))
    nq = t // tq
    q_spec = pl.BlockSpec((tq, dh), lambda bi, h, qi: (bi * nq + qi, q_col0 + h))
    k_spec = pl.BlockSpec((None, t, dh), lambda bi, h, qi: (k_seg, bi, h))
    v_spec = pl.BlockSpec((None, t, dh), lambda bi, h, qi: (v_seg, bi, h))
    o_spec = pl.BlockSpec((tq, dh), lambda bi, h, qi: (bi * nq + qi, h))
    rows = 2 * tq if kind == "diff" else tq
    scratch = _attn_scratch(rows, tq, dh)
    vm = 2 * (2 * t * dh * 2 + 2 * tq * dh * 2) + rows * (6 * tq + 4 * LANE * 4) + 8 * ATT_RG * tq * 4 + 4 * MIB
    if kind == "diff":
        slopes, lamp, gain = extra
        kern = functools.partial(_diff_prompt_kernel, tq=tq, hd=dh // 2, lam_init=lam_init)
        in_specs = [pl.BlockSpec(memory_space=pltpu.SMEM),
                    pl.BlockSpec(lamp.shape, lambda bi, h, qi: (0, 0)),
                    pl.BlockSpec(gain.shape, lambda bi, h, qi: (0, 0)),
                    q_spec, k_spec, v_spec]
        args = (slopes, lamp, gain, q, kv, kv)
    else:
        (nck,) = extra
        kern = functools.partial(_fox_prompt_kernel, tq=tq)
        in_specs = [q_spec, k_spec, v_spec,
                    pl.BlockSpec((None, None, 1, t), lambda bi, h, qi: (bi, h, 0, 0))]
        args = (q, kv, kv, nck)
    return pl.pallas_call(
        kern,
        grid=(b, nh, nq),
        in_specs=in_specs,
        out_specs=o_spec,
        out_shape=jax.ShapeDtypeStruct((b * t, nh * dh), BF),
        scratch_shapes=scratch,
        compiler_params=_cp(("parallel", "parallel", "arbitrary"), vm),
        name=kind + "_prompt_attn",
    )(*args)


def _head_state(st, h):
    s_sc, p_sc, m_sc, a_sc, acc_sc = st
    return s_sc, p_sc, m_sc.at[h], a_sc, acc_sc.at[h]


def _head_rows(c_ref, h, tk, nh):
    return c_ref[pl.ds(h, tk, stride=nh), :].astype(BF)


def _pad_keys(x, tkn):
    return x if x.shape[0] == tkn else jnp.concatenate([x, jnp.zeros((tkn - x.shape[0], x.shape[1]), x.dtype)], 0)


def _diff_sample_kernel(slopes_ref, lamp_ref, gain_ref, q_ref, kn_ref, vn_ref, kp_ref, vp_ref, o_ref, *st,
                        tq, tk, tkn, past, nh, dh, lam_init):
    kb_i = pl.program_id(1)

    @pl.when(kb_i == 0)
    def _():
        _state_init(st)

    q = q_ref[...]
    qqs = [_diff_split_q(q[:, h * dh:(h + 1) * dh], dh // 2) for h in range(nh)]
    kposf = (kb_i * tk + lax.broadcasted_iota(jnp.int32, (1, tk), 1)).astype(F32)
    for h in range(nh):
        bias = slopes_ref[h] * kposf
        _softmax_block(qqs[h], _head_rows(kp_ref, h, tk, nh), _head_rows(vp_ref, h, tk, nh),
                       lambda r0, j: _lane_tile(bias, j), None, _head_state(st, h))

    @pl.when(kb_i == pl.num_programs(1) - 1)
    def _():
        col = lax.broadcasted_iota(jnp.int32, (1, tkn), 1)
        for h in range(nh):
            hs = slice(h * dh, (h + 1) * dh)

            def diag(r0, j):
                cj = _lane_tile(col, j)
                bias, mask = _diff_bias_diag(slopes_ref[h], _chunk_rows(past + r0 % tq), past + cj)
                return bias, jnp.logical_and(mask, cj < tq)

            _softmax_block(qqs[h], _pad_keys(kn_ref[:, hs], tkn), _pad_keys(vn_ref[:, hs], tkn),
                           lambda r0, j: diag(r0, j)[0], lambda r0, j: diag(r0, j)[1], _head_state(st, h))
            o_ref[:, hs] = _diff_finish(st[4][h], lamp_ref, gain_ref, tq, lam_init).astype(o_ref.dtype)


def _fox_sample_kernel(q_ref, kn_ref, vn_ref, kp_ref, vp_ref, nckp_ref, nckn_ref, o_ref, *st, tq, tk, tkn, nh, dh):
    kb_i = pl.program_id(1)

    @pl.when(kb_i == 0)
    def _():
        _state_init(st)

    q = q_ref[...]
    for h in range(nh):
        bias = nckp_ref[h]
        _softmax_block(q[:, h * dh:(h + 1) * dh], _head_rows(kp_ref, h, tk, nh), _head_rows(vp_ref, h, tk, nh),
                       lambda r0, j: _lane_tile(bias, j), None, _head_state(st, h))

    @pl.when(kb_i == pl.num_programs(1) - 1)
    def _():
        col = lax.broadcasted_iota(jnp.int32, (1, tkn), 1)
        for h in range(nh):
            hs = slice(h * dh, (h + 1) * dh)
            bias = nckn_ref[h]
            _softmax_block(q[:, hs], _pad_keys(kn_ref[:, hs], tkn), _pad_keys(vn_ref[:, hs], tkn),
                           lambda r0, j: _lane_tile(bias, j),
                           lambda r0, j: _lane_tile(col, j) <= _chunk_rows(r0), _head_state(st, h))
            o_ref[:, hs] = _attn_out(st[4][h], dh).astype(o_ref.dtype)


def _sample_attention(kind, q, kv, cache_k, cache_v, layer, row0, b, t, nh, dh, q_col0, k_seg, v_seg, extra,
                      lam_init=None):
    past = cache_k.shape[2]
    cache_k, cache_v = (c.reshape(c.shape[0], c.shape[1], past * nh, dh) for c in (cache_k, cache_v))
    tk = _pick(past, (1024, 512, 256, 128))
    nkb = past // tk
    rb = row0 // t
    w = nh * dh
    qcb = q_col0 * dh // w
    q_spec = pl.BlockSpec((t, w), lambda bi, kb: (rb + bi, qcb))
    kn_spec = pl.BlockSpec((None, t, w), lambda bi, kb: (k_seg, rb + bi, 0))
    vn_spec = pl.BlockSpec((None, t, w), lambda bi, kb: (v_seg, rb + bi, 0))
    kp_spec = pl.BlockSpec((None, None, tk * nh, dh), lambda bi, kb: (layer, bi, kb, 0))
    o_spec = pl.BlockSpec((t, w), lambda bi, kb: (bi, 0))
    rows = 2 * t if kind == "diff" else t
    tkn = -(-t // LANE) * LANE
    scratch = _attn_scratch(rows, max(tk, tkn), dh, nh)
    vm = 2 * (2 * tk * nh * 2 * dh * 4 + 4 * t * w * 2) + 4 * nh * rows * LANE * 4 + 16 * rows * tk * 4 + 8 * MIB
    if kind == "diff":
        slopes, lamp, gain = extra
        kern = functools.partial(_diff_sample_kernel, tq=t, tk=tk, tkn=tkn, past=past, nh=nh, dh=dh,
                                 lam_init=lam_init)
        in_specs = [pl.BlockSpec(memory_space=pltpu.SMEM),
                    pl.BlockSpec(lamp.shape, lambda bi, kb: (0, 0)),
                    pl.BlockSpec(gain.shape, lambda bi, kb: (0, 0)),
                    q_spec, kn_spec, vn_spec, kp_spec, kp_spec]
        args = (slopes, lamp, gain, q, kv, kv, cache_k, cache_v)
    else:
        nckp, nckn = extra
        nckn = jnp.pad(nckn, ((0, 0), (0, 0), (0, 0), (0, tkn - t)))
        kern = functools.partial(_fox_sample_kernel, tq=t, tk=tk, tkn=tkn, nh=nh, dh=dh)
        in_specs = [q_spec, kn_spec, vn_spec, kp_spec, kp_spec,
                    pl.BlockSpec((None, nh, 1, tk), lambda bi, kb: (bi, 0, 0, kb)),
                    pl.BlockSpec((None, nh, 1, tkn), lambda bi, kb: (bi, 0, 0, 0))]
        args = (q, kv, kv, cache_k, cache_v, nckp, nckn)
    return pl.pallas_call(
        kern,
        grid=(b, nkb),
        in_specs=in_specs,
        out_specs=o_spec,
        out_shape=jax.ShapeDtypeStruct((b * t, w), BF),
        scratch_shapes=scratch,
        compiler_params=_cp(("parallel", "arbitrary"), vm),
        name=kind + "_sample_attn",
    )(*args)


def _expm1(x):
    u = jnp.exp(x)
    um1 = u - 1.0
    return jnp.where(um1 == 0.0, x, jnp.where(um1 == -1.0, -1.0, um1 * x / jnp.log(u)))


def _rglru_kernel(rx_ref, rg_ref, hist_ref, h0_ref, cw_ref, cb_ref, wa_ref, wi_ref, ba_ref, bi_ref, lam_ref,
                  yc_ref, hl_ref, cn_ref, prev_sc, h_sc, *, tt, cwidth):
    j = pl.program_id(2)

    @pl.when(j == 0)
    def _():
        prev_sc[...] = hist_ref[...]
        h_sc[...] = h0_ref[...]

    x = rx_ref[...]
    xp = jnp.concatenate([prev_sc[...], x], axis=0)
    cw = cw_ref[...]
    xc = cb_ref[...]
    for jj in range(cwidth):
        s0 = 8 - (cwidth - 1) + jj
        xc = xc + xp[s0:s0 + tt] * cw[jj:jj + 1]
    xcb = xc.astype(BF)
    r = jax.nn.sigmoid(jnp.dot(xcb, wa_ref[...], preferred_element_type=F32) + ba_ref[...])
    ig = jax.nn.sigmoid(jnp.dot(xcb, wi_ref[...], preferred_element_type=F32) + bi_ref[...])
    log_a = -LRU_C * r * jax.nn.softplus(-lam_ref[...])
    a = jnp.exp(log_a)
    bb = jnp.sqrt(-_expm1(2.0 * log_a)) * (ig * xc)

    row = lax.broadcasted_iota(jnp.int32, (tt, 1), 0)
    d = 1
    while d < tt:
        keep = row >= d
        a_sh = jnp.where(keep, pltpu.roll(a, d, 0), 1.0)
        b_sh = jnp.where(keep, pltpu.roll(bb, d, 0), 0.0)
        bb = a * b_sh + bb
        a = a * a_sh
        d *= 2
    hseq = bb + a * h_sc[...]

    yc_ref[...] = (hseq * rg_ref[...].astype(F32)).astype(yc_ref.dtype)
    h_sc[...] = hseq[tt - 1:tt]
    prev_sc[...] = xp[tt:tt + 8]

    @pl.when(j == pl.num_programs(2) - 1)
    def _():
        hl_ref[...] = hseq[tt - 1:tt]
        cn_ref[...] = xp[tt:tt + 8]


def _rglru(rx, rgg, hist8, h0, prm, layer, row0, b, t):
    c = rx.shape[1]
    cw, cb, wa, wi, ba, bi, lam = prm
    ct = wa.shape[2]
    nc = c // ct
    tt = _pick(t, (256, 128, 64, 32, 16, 8))
    nt = t // tt
    rb = row0 // tt
    cwidth = cw.shape[0]
    vec = lambda: pl.BlockSpec((1, ct), lambda bi_, ci, j: (0, ci))
    row_in = pl.BlockSpec((tt, ct), lambda bi_, ci, j: (rb + bi_ * nt + j, ci))
    wsp = pl.BlockSpec((None, None, ct, ct), lambda bi_, ci, j: (layer, ci, 0, 0))
    return pl.pallas_call(
        functools.partial(_rglru_kernel, tt=tt, cwidth=cwidth),
        grid=(b, nc, nt),
        in_specs=[row_in, row_in,
                  pl.BlockSpec((None, 8, ct), lambda bi_, ci, j: (bi_, 0, ci)),
                  pl.BlockSpec((None, 1, ct), lambda bi_, ci, j: (bi_, 0, ci)),
                  pl.BlockSpec((cwidth, ct), lambda bi_, ci, j: (0, ci)),
                  vec(), wsp, wsp, vec(), vec(), vec()],
        out_specs=[pl.BlockSpec((tt, ct), lambda bi_, ci, j: (bi_ * nt + j, ci)),
                   pl.BlockSpec((None, 1, ct), lambda bi_, ci, j: (bi_, 0, ci)),
                   pl.BlockSpec((None, 8, ct), lambda bi_, ci, j: (bi_, 0, ci))],
        out_shape=[jax.ShapeDtypeStruct((b * t, c), BF),
                   jax.ShapeDtypeStruct((b, 1, c), F32),
                   jax.ShapeDtypeStruct((b, 8, c), F32)],
        scratch_shapes=[pltpu.VMEM((8, ct), F32), pltpu.VMEM((1, ct), F32)],
        compiler_params=_cp(("parallel", "parallel", "arbitrary"), 32 * tt * ct * 4 + 8 * MIB),
        name="rglru",
    )(rx, rgg, hist8, h0, cw, cb.reshape(1, c), wa, wi, ba.reshape(1, c), bi.reshape(1, c), lam.reshape(1, c))


def _merge_kernel(x_ref, *refs, npt):
    y_refs, (wg_refs, bg_refs, wp_refs), o_ref = refs[:6], (refs[6:9], refs[9:12], refs[12:15]), refs[15]
    is_prompt = pl.program_id(0) < npt
    x = x_ref[...]
    out = None
    for k in range(3):
        y = jnp.where(is_prompt, y_refs[2 * k][...], y_refs[2 * k + 1][...])
        gate = jax.nn.sigmoid(jnp.dot(x, wg_refs[k][...], preferred_element_type=F32) + bg_refs[k][...])
        term = gate * jnp.dot(y, wp_refs[k][...], preferred_element_type=F32)
        out = term if out is None else out + term
    o_ref[...] = out.astype(o_ref.dtype)


def _merge(x, branches, w_gate, gcol0, b_gate, wps, layer):
    n, d = x.shape
    np_rows = branches[0][0].shape[0]
    tm = _pick(math.gcd(np_rows, n - np_rows), (512, 256, 128, 64, 8))
    tn = _pick(d, (512, 256, 128))
    nd = d // tn
    npt = np_rows // tm
    y_specs, ys = [], []
    for yp, ys_ in branches:
        y_specs += [pl.BlockSpec((tm, yp.shape[1]), lambda i, j: (jnp.minimum(i, npt - 1), 0)),
                    pl.BlockSpec((tm, yp.shape[1]), lambda i, j: (jnp.maximum(i - npt, 0), 0))]
        ys += [yp, ys_]
    wg_specs = [pl.BlockSpec((None, d, tn), lambda i, j, k=k: (layer, 0, gcol0 + k * nd + j)) for k in range(3)]
    bg_specs = [pl.BlockSpec((1, tn), lambda i, j, k=k: (0, k * nd + j)) for k in range(3)]
    wp_specs = [pl.BlockSpec((None, w.shape[1], tn), lambda i, j: (layer, 0, j)) for w in wps]
    bg = b_gate.reshape(1, 3 * d)
    return pl.pallas_call(
        functools.partial(_merge_kernel, npt=npt),
        grid=(n // tm, nd),
        in_specs=[pl.BlockSpec((tm, d), lambda i, j: (i, 0))] + y_specs + wg_specs + bg_specs + wp_specs,
        out_specs=pl.BlockSpec((tm, tn), lambda i, j: (i, j)),
        out_shape=jax.ShapeDtypeStruct((n, d), BF),
        compiler_params=_cp(("parallel", "parallel"), 2 * (tm * d * 2 + 3 * d * tn * 2 + 12 * tm * tn * 2) + 16 * MIB),
        name="merge",
    )(x, *ys, w_gate, w_gate, w_gate, bg, bg, bg, *wps)


def _mm_res_norm_kernel(x_ref, w_ref, h_ref, g_ref, ho_ref, xn_ref):
    hn = h_ref[...] + jnp.dot(x_ref[...], w_ref[...], preferred_element_type=F32)
    ho_ref[...] = hn
    xn_ref[...] = _rms(hn, g_ref[...]).astype(xn_ref.dtype)


def _mm_res_norm(x, w, layer, h, g, xn_dtype):
    n, k = x.shape
    d = w.shape[2]
    tm = _pick(n, (256, 128, 64, 8))
    row = lambda c: pl.BlockSpec((tm, c), lambda i: (i, 0))
    return pl.pallas_call(
        _mm_res_norm_kernel,
        grid=(n // tm,),
        in_specs=[row(k), pl.BlockSpec((None, k, d), lambda i: (layer, 0, 0)), row(d),
                  pl.BlockSpec((1, d), lambda i: (0, 0))],
        out_specs=[row(d), row(d)],
        out_shape=[jax.ShapeDtypeStruct((n, d), F32), jax.ShapeDtypeStruct((n, d), xn_dtype)],
        compiler_params=_cp(("parallel",), 2 * (k * d * 2 + tm * k * 2 + 3 * tm * d * 4) + 8 * MIB),
        name="out_proj",
    )(x, w, h, g.reshape(1, d))


def _ffn_kernel(x_ref, wg_ref, wu_ref, wd_ref, h_ref, g_ref, ho_ref, xn_ref, acc_sc):
    f = pl.program_id(1)

    @pl.when(f == 0)
    def _():
        acc_sc[...] = jnp.zeros_like(acc_sc)

    x = x_ref[...]
    a = (jax.nn.silu(jnp.dot(x, wg_ref[...], preferred_element_type=F32))
         * jnp.dot(x, wu_ref[...], preferred_element_type=F32)).astype(BF)
    acc_sc[...] += jnp.dot(a, wd_ref[...], preferred_element_type=F32)

    @pl.when(f == pl.num_programs(1) - 1)
    def _():
        hn = h_ref[...] + acc_sc[...]
        ho_ref[...] = hn
        xn_ref[...] = _rms(hn, g_ref[...]).astype(xn_ref.dtype)


def _ffn(x, wg, wu, wd, layer, h, g):
    n, d = x.shape
    ff = wg.shape[2]
    tm = _pick(n, (512, 256, 128, 64, 8))
    tf = _pick(ff, (512, 256, 128))
    row = pl.BlockSpec((tm, d), lambda i, f: (i, 0))
    vm = 2 * (3 * d * tf * 2 + tm * d * (2 + 4 + 4 + 2)) + tm * d * 4 + 3 * tm * tf * 4 + 4 * MIB
    return pl.pallas_call(
        _ffn_kernel,
        grid=(n // tm, ff // tf),
        in_specs=[row,
                  pl.BlockSpec((None, d, tf), lambda i, f: (layer, 0, f)),
                  pl.BlockSpec((None, d, tf), lambda i, f: (layer, 0, f)),
                  pl.BlockSpec((None, tf, d), lambda i, f: (layer, f, 0)),
                  row, pl.BlockSpec((1, d), lambda i, f: (0, 0))],
        out_specs=[row, row],
        out_shape=[jax.ShapeDtypeStruct((n, d), F32), jax.ShapeDtypeStruct((n, d), BF)],
        scratch_shapes=[pltpu.VMEM((tm, d), F32)],
        compiler_params=_cp(("parallel", "arbitrary"), vm),
        name="ffn",
    )(x, wg, wu, wd, h, g.reshape(1, d))


def _router_kernel(x_ref, wh_ref, wl_ref, b_ref, e_ref, p_ref, *, n_exp):
    x = x_ref[...]
    xh = x.astype(BF)
    xl = (x - xh.astype(F32)).astype(BF)
    wh = wh_ref[...]
    logits = (jnp.dot(xh, wh, preferred_element_type=F32) + jnp.dot(xl, wh, preferred_element_type=F32)
              + jnp.dot(xh, wl_ref[...], preferred_element_type=F32) + b_ref[...])
    lane = lax.broadcasted_iota(jnp.int32, logits.shape, 1)
    logits = jnp.where(lane < n_exp, logits, NEG)
    m1 = jnp.max(logits, axis=-1, keepdims=True)
    i1 = jnp.min(jnp.where(logits == m1, lane, LANE), axis=-1, keepdims=True)
    rest = jnp.where(lane == i1, NEG, logits)
    m2 = jnp.max(rest, axis=-1, keepdims=True)
    i2 = jnp.min(jnp.where(rest == m2, lane, LANE), axis=-1, keepdims=True)
    e2 = jnp.exp(m2 - m1)
    p1 = 1.0 / (1.0 + e2)
    p2 = e2 / (1.0 + e2)
    e_ref[...] = jnp.where(lane == 0, i1, jnp.where(lane == 1, i2, 0))
    p_ref[...] = jnp.where(lane == 0, p1, jnp.where(lane == 1, p2, 0.0))


def _router(x, wr, br):
    n, d = x.shape
    n_exp = wr.shape[1]
    tm = _pick(n, (256, 128, 64, 8))
    wpad = jnp.zeros((d, LANE), F32).at[:, :n_exp].set(wr)
    wh = wpad.astype(BF)
    wl = (wpad - wh.astype(F32)).astype(BF)
    bpad = jnp.zeros((1, LANE), F32).at[0, :n_exp].set(br)
    row = lambda: pl.BlockSpec((tm, LANE), lambda i: (i, 0))
    return pl.pallas_call(
        functools.partial(_router_kernel, n_exp=n_exp),
        grid=(n // tm,),
        in_specs=[pl.BlockSpec((tm, d), lambda i: (i, 0)),
                  pl.BlockSpec((d, LANE), lambda i: (0, 0)),
                  pl.BlockSpec((d, LANE), lambda i: (0, 0)),
                  pl.BlockSpec((1, LANE), lambda i: (0, 0))],
        out_specs=[row(), row()],
        out_shape=[jax.ShapeDtypeStruct((n, LANE), jnp.int32), jax.ShapeDtypeStruct((n, LANE), F32)],
        compiler_params=_cp(("parallel",), 16 * MIB),
        name="moe_router",
    )(x, wh, wl, bpad)


def _moe_ffn_kernel(te_ref, nu_ref, rt_ref, x_hbm, wg_ref, wu_ref, wd_ref, y_ref, xf_sc, xb_sc, acc_sc, sem,
                    *, tm):
    i = pl.program_id(0)
    f = pl.program_id(1)

    def row_copy(r, tok):
        return pltpu.make_async_copy(x_hbm.at[pl.ds(tok, 1)], xf_sc.at[pl.ds(r, 1)], sem.at[0])

    @pl.when(i < nu_ref[0])
    def _():
        @pl.when(f == 0)
        def _():
            for r in range(tm):
                row_copy(r, rt_ref[0, r]).start()
            for r in range(tm):
                row_copy(r, 0).wait()
            xb_sc[...] = xf_sc[...].astype(BF)
            acc_sc[...] = jnp.zeros_like(acc_sc)

        x = xb_sc[...]
        a = (jax.nn.silu(jnp.dot(x, wg_ref[...], preferred_element_type=F32))
             * jnp.dot(x, wu_ref[...], preferred_element_type=F32)).astype(BF)
        acc_sc[...] += jnp.dot(a, wd_ref[...], preferred_element_type=F32)

        @pl.when(f == pl.num_programs(1) - 1)
        def _():
            y_ref[...] = acc_sc[...]

    @pl.when(jnp.logical_and(i >= nu_ref[0], f == pl.num_programs(1) - 1))
    def _():
        y_ref[...] = jnp.zeros_like(y_ref)


def _moe_ffn(x, tile_expert, row_token, n_used, wg, wu, wd, layer, tm):
    d = x.shape[1]
    ff = wg.shape[3]
    nt = tile_expert.shape[0]
    tf = _pick(ff, (256, 128))
    nf = ff // tf

    def tile(i, nu):
        return jnp.minimum(i, nu[0] - 1)

    def fidx(i, f, nu):
        return jnp.where(i < nu[0], f, nf - 1)

    grid_spec = pltpu.PrefetchScalarGridSpec(
        num_scalar_prefetch=2,
        grid=(nt, nf),
        in_specs=[pl.BlockSpec((None, 1, tm), lambda i, f, te, nu: (i, 0, 0), memory_space=pltpu.SMEM),
                  pl.BlockSpec(memory_space=pl.ANY),
                  pl.BlockSpec((None, None, d, tf),
                               lambda i, f, te, nu: (layer, te[tile(i, nu)], 0, fidx(i, f, nu))),
                  pl.BlockSpec((None, None, d, tf),
                               lambda i, f, te, nu: (layer, te[tile(i, nu)], 0, fidx(i, f, nu))),
                  pl.BlockSpec((None, None, tf, d),
                               lambda i, f, te, nu: (layer, te[tile(i, nu)], fidx(i, f, nu), 0))],
        out_specs=pl.BlockSpec((tm, d), lambda i, f, te, nu: (i, 0)),
        scratch_shapes=[pltpu.VMEM((tm, d), F32), pltpu.VMEM((tm, d), BF), pltpu.VMEM((tm, d), F32),
                        pltpu.SemaphoreType.DMA((1,))],
    )
    vm = 2 * (3 * d * tf * 2 + tm * d * 4) + tm * d * 10 + 3 * tm * tf * 4 + 4 * MIB
    return pl.pallas_call(
        functools.partial(_moe_ffn_kernel, tm=tm),
        grid_spec=grid_spec,
        out_shape=jax.ShapeDtypeStruct((nt * tm, d), F32),
        compiler_params=_cp(("arbitrary", "arbitrary"), vm),
        name="moe_ffn",
    )(tile_expert, n_used, row_token.reshape(nt, 1, tm), x, wg, wu, wd)


def _moe_combine_kernel(pos_ref, y_hbm, p_ref, h_ref, g_ref, ho_ref, xn_ref, y_sc, sem, *, tm):
    def row_copy(k, r, src):
        return pltpu.make_async_copy(y_hbm.at[pl.ds(src, 1)], y_sc.at[k, pl.ds(r, 1)], sem.at[0])

    for r in range(tm):
        row_copy(0, r, pos_ref[0, 2 * r]).start()
        row_copy(1, r, pos_ref[0, 2 * r + 1]).start()
    for r in range(tm):
        row_copy(0, r, 0).wait()
        row_copy(1, r, 0).wait()
    p = p_ref[...]
    hn = h_ref[...] + (p[:, 0:1] * y_sc[0] + p[:, 1:2] * y_sc[1])
    ho_ref[...] = hn
    xn_ref[...] = _rms(hn, g_ref[...]).astype(xn_ref.dtype)


def _moe_combine(pos, y, p, h, g):
    n, d = h.shape
    tm = _pick(n, (256, 128, 64, 8))
    row = lambda c: pl.BlockSpec((tm, c), lambda i: (i, 0))
    return pl.pallas_call(
        functools.partial(_moe_combine_kernel, tm=tm),
        grid=(n // tm,),
        in_specs=[pl.BlockSpec((None, 1, 2 * tm), lambda i: (i, 0, 0), memory_space=pltpu.SMEM),
                  pl.BlockSpec(memory_space=pl.ANY), row(LANE), row(d),
                  pl.BlockSpec((1, d), lambda i: (0, 0))],
        out_specs=[row(d), row(d)],
        scratch_shapes=[pltpu.VMEM((2, tm, d), F32), pltpu.SemaphoreType.DMA((1,))],
        out_shape=[jax.ShapeDtypeStruct((n, d), F32), jax.ShapeDtypeStruct((n, d), BF)],
        compiler_params=_cp(("arbitrary",), 2 * (tm * d * 10 + tm * LANE * 4) + 2 * tm * d * 4 + 8 * MIB),
        name="moe_combine",
    )(pos.reshape(n // tm, 1, 2 * tm), y, p, h, g.reshape(1, d))


def _moe(xn, h, wr, br, wg, wu, wd, layer, g_next):
    n = xn.shape[0]
    n_exp = wr.shape[1]
    tm = _pick(n, (512, 256, 128, 64, 8))
    e_pad, p_pad = _router(xn, wr, br)
    top_e = e_pad[:, :2]
    onehot = (top_e[:, :, None] == jnp.arange(n_exp, dtype=jnp.int32)).astype(jnp.int32).sum(axis=1)
    counts = onehot.sum(axis=0)
    padded = ((counts + tm - 1) // tm) * tm
    ends = jnp.cumsum(padded)
    starts = ends - padded
    rank = jnp.cumsum(onehot, axis=0) - onehot
    pos = starts[top_e] + jnp.take_along_axis(rank, top_e, axis=1)
    nt = (2 * n + n_exp * (tm - 1) + tm - 1) // tm
    row_token = jnp.zeros((nt * tm,), jnp.int32).at[pos.reshape(-1)].set(
        jnp.repeat(jnp.arange(n, dtype=jnp.int32), 2))
    tile_start = jnp.arange(nt, dtype=jnp.int32) * tm
    tile_expert = jnp.minimum((tile_start[:, None] >= ends[None, :]).astype(jnp.int32).sum(axis=1), n_exp - 1)
    n_used = (ends[-1:] // tm).astype(jnp.int32)
    y = _moe_ffn(xn, tile_expert, row_token, n_used, wg, wu, wd, layer, tm)
    return _moe_combine(pos.reshape(-1).astype(jnp.int32), y, p_pad, h, g_next)


def _ple_kernel(x_ref, p_ref, wg_ref, wu_ref, h_ref, g_ref, oa_ref, ob_ref, *, npt):
    gate = jax.nn.sigmoid(jnp.dot(x_ref[...], wg_ref[...], preferred_element_type=F32))
    up = jnp.dot(p_ref[...].astype(BF), wu_ref[...], preferred_element_type=F32)
    hn = h_ref[...] + gate * up
    xn = _rms(hn, g_ref[...])
    if npt is None:
        oa_ref[...] = hn
        ob_ref[...] = xn.astype(ob_ref.dtype)
    else:
        i = pl.program_id(0)

        @pl.when(i < npt)
        def _():
            oa_ref[...] = xn

        @pl.when(i >= npt)
        def _():
            ob_ref[...] = xn


def _ple(x, p, wg, wu, layer, h, g, np_rows=None):
    n, d = x.shape
    dp = p.shape[1]
    row = lambda c: pl.BlockSpec((tm, c), lambda i: (i, 0))
    wsp = lambda a: pl.BlockSpec((None,) + a.shape[1:], lambda i: (layer, 0, 0))
    if np_rows is None:
        tm, npt = _pick(n, (256, 128, 64, 8)), None
        out_specs = [row(d), row(d)]
        out_shape = [jax.ShapeDtypeStruct((n, d), F32), jax.ShapeDtypeStruct((n, d), BF)]
    else:
        tm = _pick(math.gcd(np_rows, n - np_rows), (256, 128, 64, 8))
        npt = np_rows // tm
        out_specs = [pl.BlockSpec((tm, d), lambda i: (jnp.minimum(i, npt - 1), 0)),
                     pl.BlockSpec((tm, d), lambda i: (jnp.maximum(i - npt, 0), 0))]
        out_shape = [jax.ShapeDtypeStruct((np_rows, d), F32), jax.ShapeDtypeStruct((n - np_rows, d), F32)]
    return pl.pallas_call(
        functools.partial(_ple_kernel, npt=npt),
        grid=(n // tm,),
        in_specs=[row(d), row(dp), wsp(wg), wsp(wu), row(d), pl.BlockSpec((1, d), lambda i: (0, 0))],
        out_specs=out_specs,
        out_shape=out_shape,
        compiler_params=_cp(("arbitrary",), 2 * (d * d * 2 + dp * d * 2 + tm * d * 14 + tm * dp * 4) + 8 * MIB),
        name="ple",
    )(x, p, wg, wu, h, g.reshape(1, d))


def _block_diag(w, group):
    nl, nb, r, _ = w.shape
    eye = jnp.eye(group, dtype=w.dtype)
    return jnp.einsum("ab,lcaij->lcaibj", eye, w.reshape(nl, nb // group, group, r, r)).reshape(
        nl, nb // group, group * r, group * r)


def kernel(x_prompt, x_sample, cache_diff_k, cache_diff_v, cache_fox_k, cache_fox_v, cache_fox_logf, state_rnn_h, state_rnn_conv, p_prompt, p_sample, norm_mix, w_in, b_in, lam_q1, lam_k1, lam_q2, lam_k2, diff_gain, conv_w, conv_b, w_a, b_a, w_i, b_i, lru_lambda, w_pa, w_pb, w_pc, w_o, norm_ffn, ffn_wg, ffn_wu, ffn_wd, moe_wr, moe_br, moe_wg, moe_wu, moe_wd, ple_norm, ple_up, ple_gate, final_norm):
    bp, tp, d = x_prompt.shape
    bs, ts, _ = x_sample.shape
    depth = w_in.shape[0]
    past = cache_diff_k.shape[2]
    nh, dh = cache_diff_k.shape[3], cache_diff_k.shape[4]
    seg = nh * dh
    c_rnn = state_rnn_h.shape[2]
    cwidth = conv_w.shape[1]
    assert cache_diff_v.shape[3:] == cache_fox_k.shape[3:] == cache_fox_v.shape[3:] == (nh, dh) and dh == LANE
    assert past % (1 << CHUNK_SHIFT) == 0 and ts <= (1 << CHUNK_SHIFT) and cwidth - 1 <= 8
    np_, ns_ = bp * tp, bs * ts
    n = np_ + ns_
    assert np_ % ts == 0

    o_ff = 6 * seg
    o_rx = o_ff + nh
    assert w_in.shape[2] == o_rx + 2 * c_rnn + 3 * d

    def cols(a, lo, width):
        return lax.slice_in_dim(a, lo, lo + width, axis=a.ndim - 1)

    w_qkv = cols(w_in, 0, o_ff).astype(BF)
    w_rest = cols(w_in, o_rx, 2 * c_rnn + 3 * d).astype(BF)
    w_ff = jnp.pad(cols(w_in, o_ff, nh), ((0, 0), (0, 0), (0, LANE - nh))).astype(BF)
    w_pa16, w_pb16, w_pc16, w_o16 = (a.astype(BF) for a in (w_pa, w_pb, w_pc, w_o))
    ffn16 = tuple(a.astype(BF) for a in (ffn_wg, ffn_wu, ffn_wd))
    moe16 = tuple(a.astype(BF) for a in (moe_wg, moe_wu, moe_wd))
    ple_gate16, ple_up16 = ple_gate.astype(BF), ple_up.astype(BF)
    grp = (RNN_CT if c_rnn % RNN_CT == 0 else c_rnn) // w_a.shape[2]
    wa16, wi16 = _block_diag(w_a, grp).astype(BF), _block_diag(w_i, grp).astype(BF)

    slopes2 = jnp.asarray([LOG2E * 2.0 ** (-8.0 * (k + 1) / nh) for k in range(nh)], F32)
    q_scale = LOG2E * jnp.concatenate([jnp.full((seg,), (dh // 2) ** -0.5, F32), jnp.full((seg,), dh ** -0.5, F32)])

    w0 = 8 - (cwidth - 1)
    hist8_s = jnp.pad(state_rnn_conv, ((0, 0), (0, 0), (w0, 0), (0, 0)))
    hist8_p = jnp.zeros((bp, 8, c_rnn), F32)
    h0_p = jnp.zeros((bp, 1, c_rnn), F32)
    kv_p = tuple(jnp.zeros((depth, np_ * nh, dh), F32) for _ in range(4))
    kv_s = tuple(jnp.zeros((depth, ns_ * nh, dh), F32) for _ in range(4))

    h = jnp.concatenate([x_prompt.reshape(np_, d), x_sample.reshape(ns_, d)], axis=0)
    xn = _rmsnorm(h, norm_mix[0], BF)
    st_p, st_s = [], []
    y = None
    for i in range(depth):
        bi_l = b_in[i]
        b_q = jnp.concatenate([cols(bi_l, 0, seg), cols(bi_l, 3 * seg, seg)])
        q16 = _mm(xn, w_qkv, i, 0, 3 * (seg // _pick(2 * seg, (512, 256, 128))), b_q, BF, act="scale",
                  scale=q_scale)
        b_kv = jnp.concatenate([cols(bi_l, seg, 2 * seg), cols(bi_l, 4 * seg, 2 * seg)])
        kv16, kv_p, kv_s = _kv_proj(xn, w_qkv, i, b_kv, kv_p, kv_s, np_, nh, dh)
        b_ff = jnp.pad(cols(bi_l, o_ff, nh), (0, LANE - nh))
        logf = _mm(xn, w_ff, i, 0, 1, b_ff, F32, act="log_sigmoid")[:, :nh]
        rx = _mm(xn, w_rest, i, 0, 1, cols(bi_l, o_rx, c_rnn), F32)
        rgg = _mm(xn, w_rest, i, c_rnn // _pick(c_rnn, (512, 256, 128)), 1, cols(bi_l, o_rx + c_rnn, c_rnn), BF,
                  act="gelu")

        lam_init = 0.8 - 0.6 * math.exp(-0.3 * i)
        lamp = jnp.stack([lam_q1[i], lam_k1[i], lam_q2[i], lam_k2[i]])
        dextra = (slopes2, lamp, diff_gain[i].reshape(1, -1))
        ya_p = _prompt_attention("diff", q16, kv16, bp, tp, nh, dh, 0, 0, 1, dextra, lam_init)
        ya_s = _sample_attention("diff", q16, kv16, cache_diff_k, cache_diff_v, i, np_, bs, ts, nh, dh, 0, 0, 1,
                                 dextra, lam_init)
        logf_p = logf[:np_].reshape(bp, tp, nh)
        logf_s = logf[np_:].reshape(bs, ts, nh)
        cum_p = jnp.cumsum(logf_p, axis=1)
        cum_past = jnp.cumsum(cache_fox_logf[i].astype(F32), axis=1)
        cum_s = cum_past[:, -1:] + jnp.cumsum(logf_s, axis=1)
        as_rows = lambda c: (-LOG2E) * jnp.transpose(c, (0, 2, 1))[:, :, None, :]
        yb_p = _prompt_attention("fox", q16, kv16, bp, tp, nh, dh, nh, 2, 3, (as_rows(cum_p),))
        yb_s = _sample_attention("fox", q16, kv16, cache_fox_k, cache_fox_v, i, np_, bs, ts, nh, dh, nh, 2, 3,
                                 (as_rows(cum_past), as_rows(cum_s)))
        rprm = (conv_w[i], conv_b[i], wa16, wi16, b_a[i], b_i[i], lru_lambda[i])
        yc_p, hl_p, cn_p = _rglru(rx, rgg, hist8_p, h0_p, rprm, i, 0, bp, tp)
        yc_s, hl_s, cn_s = _rglru(rx, rgg, hist8_s[i], state_rnn_h[i][:, None, :], rprm, i, np_, bs, ts)
        merged = _merge(xn, ((ya_p, ya_s), (yb_p, yb_s), (yc_p, yc_s)), w_rest,
                        2 * c_rnn // _pick(d, (512, 256, 128)), cols(bi_l, o_rx + 2 * c_rnn, 3 * d),
                        (w_pa16, w_pb16, w_pc16), i)
        j = i // 2
        is_moe = i % 2 == 1
        h, xn2 = _mm_res_norm(merged, w_o16, i, h, norm_ffn[i], F32 if is_moe else BF)
        if is_moe:
            h, xn3 = _moe(xn2, h, moe_wr[j], moe_br[j], *moe16, j, ple_norm[i])
        else:
            h, xn3 = _ffn(xn2, *ffn16, j, h, ple_norm[i])
        p_cat = jnp.concatenate([p_prompt[i].reshape(np_, -1), p_sample[i].reshape(ns_, -1)], axis=0)
        if i == depth - 1:
            y = _ple(xn3, p_cat, ple_gate16, ple_up16, i, h, final_norm, np_)
        else:
            h, xn = _ple(xn3, p_cat, ple_gate16, ple_up16, i, h, norm_mix[i + 1])
        st_p.append((logf_p, hl_p[:, 0], cn_p[:, w0:]))
        st_s.append((logf_s, hl_s[:, 0], cn_s[:, w0:]))

    def small(sts):
        return tuple(jnp.stack([s[k] for s in sts]) for k in range(3))

    def rows5(a, b, t):
        return a.reshape(depth, b, t, nh, dh)

    return ((y[0].reshape(bp, tp, d), y[1].reshape(bs, ts, d))
            + tuple(rows5(a, bp, tp) for a in kv_p) + small(st_p)
            + tuple(rows5(a, bs, ts) for a in kv_s) + small(st_s))
```
